```python
import math
import jax, jax.numpy as jnp
from jax import lax
import numpy as np

D_MODEL = 4096
BATCH = 1
SEQ = 8192
DEPTH = 2

N_A_LAYERS = DEPTH // 2
N_B_LAYERS = DEPTH - N_A_LAYERS
D_FF = 11008
SSM_EXPAND = 2
D_INNER = SSM_EXPAND * D_MODEL
SSM_HEAD_DIM = 64
SSM_HEADS = D_INNER // SSM_HEAD_DIM
SSM_GROUPS = 8
SSM_HPG = SSM_HEADS // SSM_GROUPS
SSM_STATE = 128
CONV_WIDTH = 4
CHUNK = 128
ATTN_HEADS = 32
KV_GROUPS = 4
Q_PER_KV = ATTN_HEADS // KV_GROUPS
HEAD_DIM = 128
CMP_BLOCK = 32
CMP_STRIDE = 16
SEL_BLOCK = 64
N_SELECT = 16
WINDOW = 512
Q_BLOCK = 128
ROPE_THETA = 500000.0
ROPE_DIM = HEAD_DIM // 4
EPS = 1e-6
NEG = -1e30
FORCE_BONUS = 1e4

kernel_name = 'yoco_ssd_nsa_macaron'


def rms_norm(x, g):
    xf = x.astype(jnp.float32)
    y = xf * lax.rsqrt(jnp.mean(xf * xf, axis=-1, keepdims=True) + EPS)
    return (y * g.astype(jnp.float32)).astype(x.dtype)


def swiglu(x, w_in, w_out):
    a, b = jnp.split(x @ w_in, 2, axis=-1)
    return (jax.nn.silu(a) * b) @ w_out


def partial_rope(x, pos):
    half = ROPE_DIM // 2
    inv = ROPE_THETA ** (-jnp.arange(half, dtype=jnp.float32) / half)
    ang = pos.astype(jnp.float32)[:, None] * inv[None, :]
    cos = jnp.cos(ang)[None, :, None, :]
    sin = jnp.sin(ang)[None, :, None, :]
    xf = x.astype(jnp.float32)
    x1 = xf[..., :half]
    x2 = xf[..., half:ROPE_DIM]
    out = jnp.concatenate([x1 * cos - x2 * sin, x2 * cos + x1 * sin, xf[..., ROPE_DIM:]], axis=-1)
    return out.astype(x.dtype)


def ssd_chunked(xdt, a, bm, cm):
    b, s, g, r, p = xdt.shape
    n = bm.shape[-1]
    nc = s // CHUNK

    def to_chunks(t):
        return jnp.moveaxis(t.reshape((b, nc, CHUNK) + t.shape[2:]), 1, 0)

    causal = jnp.tril(jnp.ones((CHUNK, CHUNK), dtype=bool))[None, :, :, None, None]

    def step(state, inp):
        xc, ac, bc, cc = inp
        cum = jnp.cumsum(ac, axis=1)
        seg = cum[:, :, None] - cum[:, None, :]
        decay = jnp.exp(jnp.where(causal, seg, -jnp.inf))
        cb = jnp.einsum('blgn,bsgn->blsg', cc, bc)
        y_diag = jnp.einsum('blsgr,bsgrp->blgrp', cb[..., None] * decay, xc)
        y_off = jnp.einsum('blgn,bgrpn->blgrp', cc, state) * jnp.exp(cum)[..., None]
        to_end = jnp.exp(cum[:, -1:] - cum)
        new_state = state * jnp.exp(cum[:, -1])[..., None, None] + jnp.einsum(
            'bsgn,bsgrp->bgrpn', bc, xc * to_end[..., None])
        return new_state, y_diag + y_off

    state0 = jnp.zeros((b, g, r, p, n), jnp.float32)
    _, y = lax.scan(step, state0, (to_chunks(xdt), to_chunks(a), to_chunks(bm), to_chunks(cm)))
    return jnp.moveaxis(y, 0, 1).reshape(b, s, g, r, p)


def mamba2_mixer(u, w_in, conv_w, conv_b, dt_bias, a_log, d_skip, norm_g, w_out):
    b, s, _ = u.shape
    gn = SSM_GROUPS * SSM_STATE
    conv_ch = D_INNER + 2 * gn
    zxbcdt = u @ w_in
    z = zxbcdt[..., :D_INNER]
    xbc = zxbcdt[..., D_INNER:D_INNER + conv_ch]
    dt = zxbcdt[..., D_INNER + conv_ch:]
    xbc = lax.conv_general_dilated(
        xbc, conv_w[:, None, :], window_strides=(1,), padding=[(CONV_WIDTH - 1, 0)],
        dimension_numbers=('NWC', 'WIO', 'NWC'), feature_group_count=conv_ch) + conv_b
    xbc = jax.nn.silu(xbc)
    xs = xbc[..., :D_INNER].reshape(b, s, SSM_GROUPS, SSM_HPG, SSM_HEAD_DIM).astype(jnp.float32)
    bm = xbc[..., D_INNER:D_INNER + gn].reshape(b, s, SSM_GROUPS, SSM_STATE).astype(jnp.float32)
    cm = xbc[..., D_INNER + gn:].reshape(b, s, SSM_GROUPS, SSM_STATE).astype(jnp.float32)
    dt = jax.nn.softplus(dt.astype(jnp.float32) + dt_bias.astype(jnp.float32))
    dt = dt.reshape(b, s, SSM_GROUPS, SSM_HPG)
    a = -jnp.exp(a_log.astype(jnp.float32)).reshape(SSM_GROUPS, SSM_HPG)
    y = ssd_chunked(xs * dt[..., None], dt * a, bm, cm)
    y = y + d_skip.astype(jnp.float32).reshape(SSM_GROUPS, SSM_HPG)[..., None] * xs
    y = y.reshape(b, s, D_INNER) * jax.nn.silu(z.astype(jnp.float32))
    yg = y.reshape(b, s, SSM_GROUPS, D_INNER // SSM_GROUPS)
    yg = yg * lax.rsqrt(jnp.mean(yg * yg, axis=-1, keepdims=True) + EPS)
    y = yg.reshape(b, s, D_INNER) * norm_g.astype(jnp.float32)
    return y.astype(u.dtype) @ w_out


def shared_kv(h, kv_norm, kv_w, cmp_pe_k, cmp_w1_k, cmp_w2_k, cmp_pe_v, cmp_w1_v, cmp_w2_v,
              k_norm_cmp, k_norm_slc, k_norm_win):
    b, s, _ = h.shape
    kv = (rms_norm(h, kv_norm) @ kv_w).reshape(b, s, 6, KV_GROUPS, HEAD_DIM)
    k_c, v_c, k_s, v_s, k_w, v_w = (kv[:, :, i] for i in range(6))
    pos = jnp.arange(s)
    n_cmp = (s - CMP_BLOCK) // CMP_STRIDE + 1
    idx = jnp.arange(n_cmp)[:, None] * CMP_STRIDE + jnp.arange(CMP_BLOCK)[None, :]

    def compress(t, pe, w1, w2):
        blk = t[:, idx] + pe[None, None, :, None, :]
        blk = jnp.moveaxis(blk, 3, 2).reshape(b, n_cmp, KV_GROUPS, CMP_BLOCK * HEAD_DIM)
        return jax.nn.silu(blk @ w1) @ w2

    kc = partial_rope(rms_norm(compress(k_c, cmp_pe_k, cmp_w1_k, cmp_w2_k), k_norm_cmp), idx[:, -1])
    vc = compress(v_c, cmp_pe_v, cmp_w1_v, cmp_w2_v)
    ks = partial_rope(rms_norm(k_s, k_norm_slc), pos)
    kw = partial_rope(rms_norm(k_w, k_norm_win), pos)
    return kc, vc, ks, v_s, kw, v_w


def nsa_mixer(u, w_qg, q_norm_g, w_o, kc, vc, ks, vs, kw, vw):
    b, s, _ = u.shape
    qg = u @ w_qg
    q = qg[..., :ATTN_HEADS * HEAD_DIM].reshape(b, s, ATTN_HEADS, HEAD_DIM)
    gates = jax.nn.sigmoid(qg[..., ATTN_HEADS * HEAD_DIM:].astype(jnp.float32))
    gates = gates.reshape(b, s, KV_GROUPS, Q_PER_KV, 3)
    q = partial_rope(rms_norm(q, q_norm_g), jnp.arange(s)).reshape(b, s, KV_GROUPS, Q_PER_KV, HEAD_DIM)
    scale = HEAD_DIM ** -0.5

    n_cmp = kc.shape[1]
    cmp_end = jnp.arange(n_cmp) * CMP_STRIDE + CMP_BLOCK - 1
    n_sblk = s // SEL_BLOCK
    n_top = min(N_SELECT, n_sblk)
    ratio = SEL_BLOCK // CMP_STRIDE
    offs = np.arange(-(CMP_BLOCK // CMP_STRIDE), ratio + 1)
    ov = np.clip(np.minimum(offs * CMP_STRIDE + CMP_BLOCK, SEL_BLOCK) - np.maximum(offs * CMP_STRIDE, 0), 0, None)
    keep = ov > 0
    offs, ov = offs[keep], ov[keep]
    sel_w = jnp.asarray(ov / CMP_BLOCK, dtype=jnp.float32)
    pad_l = int(-offs.min())
    sel_idx = np.arange(n_sblk)[:, None] * ratio + offs[None, :] + pad_l
    pad_r = max(0, int(sel_idx.max()) + 1 - (n_cmp + pad_l))
    sel_idx = jnp.asarray(sel_idx, dtype=jnp.int32)

    ks_blk = jnp.moveaxis(ks.reshape(b, n_sblk, SEL_BLOCK, KV_GROUPS, HEAD_DIM), 3, 1)
    vs_blk = jnp.moveaxis(vs.reshape(b, n_sblk, SEL_BLOCK, KV_GROUPS, HEAD_DIM), 3, 1)
    bi = jnp.arange(b)[:, None, None, None]
    gi = jnp.arange(KV_GROUPS)[None, :, None, None]
    sblk_ids = jnp.arange(n_sblk)
    win_len = WINDOW + Q_BLOCK
    kw_pad = jnp.pad(kw, ((0, 0), (WINDOW, 0), (0, 0), (0, 0)))
    vw_pad = jnp.pad(vw, ((0, 0), (WINDOW, 0), (0, 0), (0, 0)))

    def block(args):
        qb_idx, qblk, gblk = args
        t = qb_idx * Q_BLOCK + jnp.arange(Q_BLOCK)
        sc = jnp.einsum('bqgrd,bcgd->bgrqc', qblk, kc).astype(jnp.float32) * scale
        mc = cmp_end[None, :] <= t[:, None]
        pc = jax.nn.softmax(jnp.where(mc, sc, NEG), axis=-1) * mc
        o_c = jnp.einsum('bgrqc,bcgd->bqgrd', pc.astype(vc.dtype), vc)
        imp = jnp.pad(pc.sum(axis=2), ((0, 0), (0, 0), (0, 0), (pad_l, pad_r)))
        imp = jnp.einsum('bgqjo,o->bgqj', imp[..., sel_idx], sel_w)
        cur = t // SEL_BLOCK
        valid = sblk_ids[None, :] <= cur[:, None]
        forced = (sblk_ids[None, :] == 0) | (sblk_ids[None, :] == cur[:, None]) | (sblk_ids[None, :] == cur[:, None] - 1)
        score = jnp.where(valid, imp + jnp.where(forced, FORCE_BONUS, 0.0), NEG)
        _, top = lax.top_k(score, n_top)
        k_g = ks_blk[bi, gi, top]
        v_g = vs_blk[bi, gi, top]
        tok = top[..., None] * SEL_BLOCK + jnp.arange(SEL_BLOCK)
        ms = tok <= t[None, None, :, None, None]
        ss = jnp.einsum('bqgrd,bgqnkd->bgrqnk', qblk, k_g).astype(jnp.float32) * scale
        ss = jnp.where(ms[:, :, None], ss, NEG).reshape(b, KV_GROUPS, Q_PER_KV, Q_BLOCK, n_top * SEL_BLOCK)
        ps = jax.nn.softmax(ss, axis=-1).reshape(b, KV_GROUPS, Q_PER_KV, Q_BLOCK, n_top, SEL_BLOCK)
        o_s = jnp.einsum('bgrqnk,bgqnkd->bqgrd', ps.astype(v_g.dtype), v_g)
        start = qb_idx * Q_BLOCK
        kwin = lax.dynamic_slice_in_dim(kw_pad, start, win_len, axis=1)
        vwin = lax.dynamic_slice_in_dim(vw_pad, start, win_len, axis=1)
        spos = start - WINDOW + jnp.arange(win_len)
        mw = (spos[None, :] <= t[:, None]) & (spos[None, :] > t[:, None] - WINDOW) & (spos[None, :] >= 0)
        sw = jnp.einsum('bqgrd,bkgd->bgrqk', qblk, kwin).astype(jnp.float32) * scale
        pw = jax.nn.softmax(jnp.where(mw, sw, NEG), axis=-1)
        o_w = jnp.einsum('bgrqk,bkgd->bqgrd', pw.astype(vwin.dtype), vwin)
        o = gblk[..., 0:1] * o_c + gblk[..., 1:2] * o_s + gblk[..., 2:3] * o_w
        return o.astype(u.dtype)

    nqb = s // Q_BLOCK
    q_blocks = jnp.moveaxis(q.reshape(b, nqb, Q_BLOCK, KV_GROUPS, Q_PER_KV, HEAD_DIM), 1, 0)
    g_blocks = jnp.moveaxis(gates.reshape(b, nqb, Q_BLOCK, KV_GROUPS, Q_PER_KV, 3), 1, 0)
    o = lax.map(block, (jnp.arange(nqb), q_blocks, g_blocks))
    o = jnp.moveaxis(o, 0, 1).reshape(b, s, ATTN_HEADS * HEAD_DIM)
    return o @ w_o


def setup_inputs(seed: int = 0) -> dict:
    key = jax.random.key(seed)
    ks = jax.random.split(key, 32)
    f32 = jnp.float32

    def nrm(k, shape, fan_in):
        return jax.random.normal(k, shape, f32) * (fan_in ** -0.5)

    def gain(k, shape):
        return 1.0 + 0.02 * jax.random.normal(k, shape, f32)

    gn = SSM_GROUPS * SSM_STATE
    conv_ch = D_INNER + 2 * gn
    in_cols = 2 * D_INNER + 2 * gn + SSM_HEADS
    dt0 = jnp.exp(jax.random.uniform(ks[10], (N_A_LAYERS, SSM_HEADS), f32, math.log(1e-3), math.log(1e-1)))
    return {
        'x': jax.random.normal(ks[0], (BATCH, SEQ, D_MODEL), f32),
        'ffn_a_norm': gain(ks[1], (DEPTH, D_MODEL)),
        'ffn_a_w_in': nrm(ks[2], (DEPTH, D_MODEL, 2 * D_FF), D_MODEL),
        'ffn_a_w_out': nrm(ks[3], (DEPTH, D_FF, D_MODEL), D_FF),
        'ffn_b_norm': gain(ks[4], (DEPTH, D_MODEL)),
        'ffn_b_w_in': nrm(ks[5], (DEPTH, D_MODEL, 2 * D_FF), D_MODEL),
        'ffn_b_w_out': nrm(ks[6], (DEPTH, D_FF, D_MODEL), D_FF),
        'mix_norm': gain(ks[7], (DEPTH, D_MODEL)),
        'ssm_w_in': nrm(ks[8], (N_A_LAYERS, D_MODEL, in_cols), D_MODEL),
        'ssm_conv_w': nrm(ks[9], (N_A_LAYERS, CONV_WIDTH, conv_ch), CONV_WIDTH),
        'ssm_conv_b': 0.02 * jax.random.normal(ks[11], (N_A_LAYERS, conv_ch), f32),
        'ssm_dt_bias': dt0 + jnp.log(-jnp.expm1(-dt0)),
        'ssm_a_log': jnp.log(jax.random.uniform(ks[12], (N_A_LAYERS, SSM_HEADS), f32, 1.0, 16.0)),
        'ssm_d': gain(ks[13], (N_A_LAYERS, SSM_HEADS)),
        'ssm_norm': gain(ks[14], (N_A_LAYERS, D_INNER)),
        'ssm_w_out': nrm(ks[15], (N_A_LAYERS, D_INNER, D_MODEL), D_INNER),
        'kv_norm': gain(ks[16], (D_MODEL,)),
        'kv_w': nrm(ks[17], (D_MODEL, 6 * KV_GROUPS * HEAD_DIM), D_MODEL),
        'cmp_pe_k': 0.02 * jax.random.normal(ks[18], (CMP_BLOCK, HEAD_DIM), f32),
        'cmp_w1_k': nrm(ks[19], (CMP_BLOCK * HEAD_DIM, HEAD_DIM), CMP_BLOCK * HEAD_DIM),
        'cmp_w2_k': nrm(ks[20], (HEAD_DIM, HEAD_DIM), HEAD_DIM),
        'cmp_pe_v': 0.02 * jax.random.normal(ks[21], (CMP_BLOCK, HEAD_DIM), f32),
        'cmp_w1_v': nrm(ks[22], (CMP_BLOCK * HEAD_DIM, HEAD_DIM), CMP_BLOCK * HEAD_DIM),
        'cmp_w2_v': nrm(ks[23], (HEAD_DIM, HEAD_DIM), HEAD_DIM),
        'k_norm_cmp': gain(ks[24], (HEAD_DIM,)),
        'k_norm_slc': gain(ks[25], (HEAD_DIM,)),
        'k_norm_win': gain(ks[26], (HEAD_DIM,)),
        'attn_w_qg': nrm(ks[27], (N_B_LAYERS, D_MODEL, ATTN_HEADS * HEAD_DIM + 3 * ATTN_HEADS), D_MODEL),
        'attn_q_norm': gain(ks[28], (N_B_LAYERS, HEAD_DIM)),
        'attn_w_o': nrm(ks[29], (N_B_LAYERS, ATTN_HEADS * HEAD_DIM, D_MODEL), ATTN_HEADS * HEAD_DIM),
    }


def reference(x, ffn_a_norm, ffn_a_w_in, ffn_a_w_out, ffn_b_norm, ffn_b_w_in, ffn_b_w_out, mix_norm,
              ssm_w_in, ssm_conv_w, ssm_conv_b, ssm_dt_bias, ssm_a_log, ssm_d, ssm_norm, ssm_w_out,
              kv_norm, kv_w, cmp_pe_k, cmp_w1_k, cmp_w2_k, cmp_pe_v, cmp_w1_v, cmp_w2_v,
              k_norm_cmp, k_norm_slc, k_norm_win, attn_w_qg, attn_q_norm, attn_w_o):
    h = x
    kv_shared = None
    for i in range(DEPTH):
        h = h + 0.5 * swiglu(rms_norm(h, ffn_a_norm[i]), ffn_a_w_in[i], ffn_a_w_out[i])
        u = rms_norm(h, mix_norm[i])
        if i < N_A_LAYERS:
            h = h + mamba2_mixer(u, ssm_w_in[i], ssm_conv_w[i], ssm_conv_b[i], ssm_dt_bias[i],
                                 ssm_a_log[i], ssm_d[i], ssm_norm[i], ssm_w_out[i])
        else:
            j = i - N_A_LAYERS
            kc, vc, ks, vs, kw, vw = kv_shared
            h = h + nsa_mixer(u, attn_w_qg[j], attn_q_norm[j], attn_w_o[j], kc, vc, ks, vs, kw, vw)
        h = h + 0.5 * swiglu(rms_norm(h, ffn_b_norm[i]), ffn_b_w_in[i], ffn_b_w_out[i])
        if i == N_A_LAYERS - 1:
            kv_shared = shared_kv(h, kv_norm, kv_w, cmp_pe_k, cmp_w1_k, cmp_w2_k, cmp_pe_v, cmp_w1_v,
                                  cmp_w2_v, k_norm_cmp, k_norm_slc, k_norm_win)
    return h
```

```python
import functools
import math

import jax
import jax.numpy as jnp
import numpy as np
from jax import lax
from jax.experimental import pallas as pl
from jax.experimental.pallas import tpu as pltpu

F32 = jnp.float32
BF16 = jnp.bfloat16

D_MODEL = 4096
D_FF = 11008
D_INNER = 8192
SSM_HEAD_DIM = 64
SSM_GROUPS = 8
SSM_HPG = 16
SSM_STATE = 128
SSM_GROUP_W = SSM_HPG * SSM_HEAD_DIM
CONV_WIDTH = 4
CHUNK = 128
ATTN_HEADS = 32
KV_GROUPS = 4
Q_PER_KV = 8
HEAD_DIM = 128
CMP_BLOCK = 32
CMP_STRIDE = 16
SEL_BLOCK = 64
N_SELECT = 16
WINDOW = 512
Q_BLOCK = 128
ROPE_THETA = 500000.0
ROPE_DIM = 32
EPS = 1e-6
NEG = -1e30
FORCE_BONUS = 1e4

LANES = 128
VMEM_LIMIT = 56 * 1024 * 1024


def _params(sem):
    return pltpu.CompilerParams(dimension_semantics=sem, vmem_limit_bytes=VMEM_LIMIT)


def _dot(a, b):
    return jnp.dot(a, b, preferred_element_type=F32)


def _dot_nt(a, b):
    return lax.dot_general(a, b, (((1,), (1,)), ((), ())), preferred_element_type=F32)


def _dot_tn(a, b):
    return lax.dot_general(a, b, (((0,), (0,)), ((), ())), preferred_element_type=F32)


def _split3(x):
    hi = x.astype(BF16)
    r1 = x - hi.astype(F32)
    mid = r1.astype(BF16)
    lo = (r1 - mid.astype(F32)).astype(BF16)
    return hi, mid, lo


def _sigmoid(x):
    return 1.0 / (1.0 + jnp.exp(-x))


def _silu(x):
    return x * _sigmoid(x)


def _softplus(x):
    return jnp.maximum(x, 0.0) + jnp.log1p(jnp.exp(-jnp.abs(x)))


def _dot_cast(a_ref, w_ref, k_chunk):
    k = a_ref.shape[1]
    acc = None
    for k0 in range(0, k, k_chunk):
        part = _dot(a_ref[:, k0:k0 + k_chunk], w_ref[k0:k0 + k_chunk, :].astype(BF16))
        acc = part if acc is None else acc + part
    return acc


def _rmsnorm_kernel(x_ref, g_ref, o_ref):
    x = x_ref[...]
    ms = jnp.mean(x * x, axis=-1, keepdims=True)
    o_ref[...] = (x * lax.rsqrt(ms + EPS) * g_ref[...]).astype(o_ref.dtype)


def rmsnorm(x, g, tm=256):
    s, d = x.shape
    return pl.pallas_call(
        _rmsnorm_kernel,
        grid=(s // tm,),
        in_specs=[pl.BlockSpec((tm, d), lambda i: (i, 0)),
                  pl.BlockSpec((1, d), lambda i: (0, 0))],
        out_specs=pl.BlockSpec((tm, d), lambda i: (i, 0)),
        out_shape=jax.ShapeDtypeStruct((s, d), BF16),
        compiler_params=_params(("parallel",)),
        name="rmsnorm",
    )(x, g.reshape(1, d))


def _row_tile(s, want):
    return min(s, want)


def _ffn_in_kernel(x_ref, wa_ref, wb_ref, o_ref):
    x = x_ref[...]
    a = _dot(x, wa_ref[...].astype(BF16))
    b = _dot(x, wb_ref[...].astype(BF16))
    o_ref[...] = (_silu(a) * b).astype(o_ref.dtype)


def ffn_in(xn, w_in, tm=1024, tn=256):
    s, k = xn.shape
    f = w_in.shape[1] // 2
    tm = _row_tile(s, tm)
    nb = f // tn
    return pl.pallas_call(
        _ffn_in_kernel,
        grid=(s // tm, nb),
        in_specs=[pl.BlockSpec((tm, k), lambda i, j: (i, 0)),
                  pl.BlockSpec((k, tn), lambda i, j: (0, j)),
                  pl.BlockSpec((k, tn), lambda i, j: (0, j + nb))],
        out_specs=pl.BlockSpec((tm, tn), lambda i, j: (i, j)),
        out_shape=jax.ShapeDtypeStruct((s, f), BF16),
        compiler_params=_params(("parallel", "arbitrary")),
        name="ffn_in",
    )(xn, w_in, w_in)


def _k_chunk(k):
    for c in range(min(k, 5632), 0, -LANES):
        if c % LANES == 0 and k % c == 0:
            return c
    return k


def _mm_res_kernel(a_ref, w_ref, r_ref, o_ref, *, scale, k_chunk):
    acc = _dot_cast(a_ref, w_ref, k_chunk)
    o_ref[...] = r_ref[...] + scale * acc


def mm_res(a, w, res, scale, tm=1024, tn=256):
    s, k = a.shape
    n = w.shape[1]
    tm = _row_tile(s, tm)
    return pl.pallas_call(
        functools.partial(_mm_res_kernel, scale=scale, k_chunk=_k_chunk(k)),
        grid=(s // tm, n // tn),
        in_specs=[pl.BlockSpec((tm, k), lambda i, j: (i, 0), pipeline_mode=pl.Buffered(1)),
                  pl.BlockSpec((k, tn), lambda i, j: (0, j)),
                  pl.BlockSpec((tm, tn), lambda i, j: (i, j))],
        out_specs=pl.BlockSpec((tm, tn), lambda i, j: (i, j)),
        out_shape=jax.ShapeDtypeStruct((s, n), F32),
        compiler_params=_params(("parallel", "arbitrary")),
        name="mm_res",
    )(a, w, res)


def _mm_plain_kernel(a_ref, w_ref, o_ref, *, act):
    acc = _dot(a_ref[...], w_ref[...].astype(BF16))
    if act == "sigmoid":
        acc = _sigmoid(acc)
    o_ref[...] = acc.astype(o_ref.dtype)


def mm_plain(a, w, n, tn, tm=1024, act=None, out_dtype=F32):
    s, k = a.shape
    tm = _row_tile(s, tm)
    return pl.pallas_call(
        functools.partial(_mm_plain_kernel, act=act),
        grid=(s // tm, n // tn),
        in_specs=[pl.BlockSpec((tm, k), lambda i, j: (i, 0)),
                  pl.BlockSpec((k, tn), lambda i, j: (0, j))],
        out_specs=pl.BlockSpec((tm, tn), lambda i, j: (i, j)),
        out_shape=jax.ShapeDtypeStruct((s, n), out_dtype),
        compiler_params=_params(("parallel", "arbitrary")),
        name="mm_plain",
    )(a, w)


def rope_tables(pos):
    half = ROPE_DIM // 2
    inv = ROPE_THETA ** (-jnp.arange(half, dtype=F32) / half)
    ang = pos.astype(F32)[:, None] * inv[None, :]
    cos, sin = jnp.cos(ang), jnp.sin(ang)
    n = pos.shape[0]
    ones = jnp.ones((n, HEAD_DIM - ROPE_DIM), F32)
    zeros = jnp.zeros((n, HEAD_DIM - ROPE_DIM), F32)
    z16 = jnp.zeros((n, half), F32)
    cos_t = jnp.concatenate([cos, cos, ones], axis=1)
    sin_a = jnp.concatenate([z16, sin, zeros], axis=1)
    sin_b = jnp.concatenate([-sin, z16, zeros], axis=1)
    return cos_t, sin_a, sin_b


def _norm_rope(x, g, cos_t, sin_a, sin_b):
    ms = jnp.mean(x * x, axis=-1, keepdims=True)
    y = x * lax.rsqrt(ms + EPS) * g
    half = ROPE_DIM // 2
    return (y * cos_t + pltpu.roll(y, half, 1) * sin_a
            + pltpu.roll(y, HEAD_DIM - half, 1) * sin_b)


def _head_norm_rope_kernel(x_ref, g_ref, c_ref, sa_ref, sb_ref, o_ref, *, nh, scale):
    g = g_ref[...]
    c, sa, sb = c_ref[...], sa_ref[...], sb_ref[...]
    for h in range(nh):
        x = x_ref[:, h * HEAD_DIM:(h + 1) * HEAD_DIM]
        y = _norm_rope(x, g, c, sa, sb)
        o_ref[:, h * HEAD_DIM:(h + 1) * HEAD_DIM] = (y * scale).astype(o_ref.dtype)


def head_norm_rope(x, col0, nh, g, tables, scale=1.0, tm=256):
    s = x.shape[0]
    w = nh * HEAD_DIM
    cb = col0 // w
    tm = _row_tile(s, tm)
    tab_spec = pl.BlockSpec((tm, HEAD_DIM), lambda i: (i, 0))
    return pl.pallas_call(
        functools.partial(_head_norm_rope_kernel, nh=nh, scale=scale),
        grid=(s // tm,),
        in_specs=[pl.BlockSpec((tm, w), lambda i: (i, cb)),
                  pl.BlockSpec((1, HEAD_DIM), lambda i: (0, 0)),
                  tab_spec, tab_spec, tab_spec],
        out_specs=pl.BlockSpec((tm, w), lambda i: (i, 0)),
        out_shape=jax.ShapeDtypeStruct((s, w), BF16),
        compiler_params=_params(("parallel",)),
        name="head_norm_rope",
    )(x, g.reshape(1, HEAD_DIM), *tables)


def _compress_kernel(t_ref, pe_ref, w1_ref, w2_ref, g_ref, c_ref, sa_ref, sb_ref, o_ref, *, is_key):
    half = CMP_STRIDE * HEAD_DIM
    t = t_ref[0]
    top = (t + pe_ref[:, :half]).astype(BF16)
    bot = (t + pe_ref[:, half:]).astype(BF16)
    p = _dot(top, w1_ref[:half, :].astype(BF16))
    q = _dot(bot, w1_ref[half:, :].astype(BF16))
    n = t.shape[0]
    hid = p + pltpu.roll(q, n - 1, 0)
    out = _dot(_silu(hid).astype(BF16), w2_ref[...].astype(BF16))
    if is_key:
        out = _norm_rope(out, g_ref[...], c_ref[...], sa_ref[...], sb_ref[...])
    o_ref[0] = out.astype(o_ref.dtype)


def compress(t16, pe, w1, w2, g, tables, is_key):
    ng, n, wid = t16.shape
    tab_spec = pl.BlockSpec((n, HEAD_DIM), lambda i: (0, 0))
    return pl.pallas_call(
        functools.partial(_compress_kernel, is_key=is_key),
        grid=(ng,),
        in_specs=[pl.BlockSpec((1, n, wid), lambda i: (i, 0, 0)),
                  pl.BlockSpec((1, 2 * wid), lambda i: (0, 0)),
                  pl.BlockSpec((2 * wid, HEAD_DIM), lambda i: (0, 0)),
                  pl.BlockSpec((HEAD_DIM, HEAD_DIM), lambda i: (0, 0)),
                  pl.BlockSpec((1, HEAD_DIM), lambda i: (0, 0)),
                  tab_spec, tab_spec, tab_spec],
        out_specs=pl.BlockSpec((1, n, HEAD_DIM), lambda i: (i, 0, 0)),
        out_shape=jax.ShapeDtypeStruct((ng, n, HEAD_DIM), BF16),
        compiler_params=_params(("parallel",)),
        name="compress_k" if is_key else "compress_v",
    )(t16, pe.reshape(1, 2 * wid), w1, w2, g.reshape(1, HEAD_DIM), *tables)


def _conv_silu(x, carry_ref, w_ref, b_ref, first):
    @pl.when(first)
    def _():
        carry_ref[...] = jnp.zeros_like(carry_ref)

    full = jnp.concatenate([carry_ref[...], x], axis=0)
    n = x.shape[0]
    acc = b_ref[...] + w_ref[CONV_WIDTH - 1:CONV_WIDTH, :] * x
    for k in range(1, CONV_WIDTH):
        acc = acc + w_ref[CONV_WIDTH - 1 - k:CONV_WIDTH - k, :] * full[8 - k:8 - k + n]
    carry_ref[...] = x[n - 8:, :]
    return _silu(acc)


def _ssd_kernel(z_ref, x_ref, b_ref, c_ref, dt_ref, dtt_ref,
                cwx_ref, cwb_ref, cwc_ref, cbx_ref, cbb_ref, cbc_ref,
                bias_ref, biast_ref, alog_ref, alogt_ref, d_ref, ng_ref,
                o_ref,
                state_ref, xs_ref, y_ref, cx_ref, cb_ref, cc_ref):
    ci = pl.program_id(1)
    first = ci == 0
    L = CHUNK

    @pl.when(first)
    def _():
        state_ref[...] = jnp.zeros_like(state_ref)

    xg = _conv_silu(x_ref[...], cx_ref, cwx_ref, cbx_ref, first)
    bm = _conv_silu(b_ref[...], cb_ref, cwb_ref, cbb_ref, first)
    cm = _conv_silu(c_ref[...], cc_ref, cwc_ref, cbc_ref, first)

    dt = _softplus(dt_ref[0] + bias_ref[0])
    dtt = _softplus(dtt_ref[0] + biast_ref[0])
    a = dt * (-jnp.exp(alog_ref[0]))
    at = dtt * (-jnp.exp(alogt_ref[0]))

    row = lax.broadcasted_iota(jnp.int32, (L, L), 0)
    col = lax.broadcasted_iota(jnp.int32, (L, L), 1)
    causal = row >= col
    tril = causal.astype(BF16)
    triu = (row <= col).astype(BF16)
    cum = sum(_dot(tril, p) for p in _split3(a))
    cumt = sum(_dot(p, triu) for p in _split3(at))
    ecum = jnp.exp(cum)
    last = cum[L - 1:L, :]
    toend = jnp.exp(last - cum)
    elast = jnp.exp(last)

    bmb = bm.astype(BF16)
    cmb = cm.astype(BF16)
    cbmat = _dot_nt(cmb, bmb)
    yoff = _dot(cmb, state_ref[...].astype(BF16))

    lane = lax.broadcasted_iota(jnp.int32, (L, LANES), 1)
    lo = lane < SSM_HEAD_DIM
    lane1 = lax.broadcasted_iota(jnp.int32, (1, LANES), 1)
    lo1 = lane1 < SSM_HEAD_DIM
    dsk = d_ref[0]
    for k in range(SSM_HPG // 2):
        h0, h1 = 2 * k, 2 * k + 1
        sl = slice(k * LANES, (k + 1) * LANES)
        xp = xg[:, sl]
        xdt = xp * jnp.where(lo, dt[:, h0:h0 + 1], dt[:, h1:h1 + 1])
        g0 = (cbmat * jnp.exp(jnp.where(causal, cum[:, h0:h0 + 1] - cumt[h0:h0 + 1, :], -jnp.inf))).astype(BF16)
        g1 = (cbmat * jnp.exp(jnp.where(causal, cum[:, h1:h1 + 1] - cumt[h1:h1 + 1, :], -jnp.inf))).astype(BF16)
        ydiag = (_dot(g0, jnp.where(lo, xdt, 0.0).astype(BF16))
                 + _dot(g1, jnp.where(lo, 0.0, xdt).astype(BF16)))
        y = ydiag + yoff[:, sl] * jnp.where(lo, ecum[:, h0:h0 + 1], ecum[:, h1:h1 + 1])
        y = y + jnp.where(lo1, dsk[:, h0:h0 + 1], dsk[:, h1:h1 + 1]) * xp
        y_ref[:, sl] = y * _silu(z_ref[:, sl])
        xs_ref[:, sl] = (xdt * jnp.where(lo, toend[:, h0:h0 + 1], toend[:, h1:h1 + 1])).astype(BF16)
        state_ref[:, sl] = state_ref[:, sl] * jnp.where(lo1, elast[:, h0:h0 + 1], elast[:, h1:h1 + 1])

    state_ref[...] += _dot_tn(bmb, xs_ref[...])

    y = y_ref[...]
    ms = jnp.mean(y * y, axis=-1, keepdims=True)
    o_ref[...] = (y * lax.rsqrt(ms + EPS) * ng_ref[...]).astype(o_ref.dtype)


def ssd_mixer_core(zx, bcdt, conv_w, conv_b, dt_bias, a_log, d_skip, norm_g):
    s = zx.shape[0]
    nc = s // CHUNK
    gw, n, hpg, ng = SSM_GROUP_W, SSM_STATE, SSM_HPG, SSM_GROUPS
    dt_raw = bcdt[:, 2 * ng * n:2 * ng * n + ng * hpg]
    pad = ((0, 0), (0, 0), (0, LANES - hpg))
    dt_g = jnp.pad(dt_raw.reshape(s, ng, hpg).transpose(1, 0, 2), pad)
    dt_gt = dt_raw.reshape(s, ng, hpg).transpose(1, 2, 0)

    def per_head(v):
        return jnp.pad(v.reshape(ng, 1, hpg), pad), v.reshape(ng, hpg, 1)

    bias, biast = per_head(dt_bias)
    alog, alogt = per_head(a_log)
    dsk, _ = per_head(d_skip)
    cb2 = conv_b.reshape(1, -1)
    xoff, boff, coff = D_INNER // gw, D_INNER // n, D_INNER // n + ng
    head_spec = pl.BlockSpec((1, 1, LANES), lambda g, c: (g, 0, 0))
    headt_spec = pl.BlockSpec((1, hpg, 1), lambda g, c: (g, 0, 0))
    return pl.pallas_call(
        _ssd_kernel,
        grid=(ng, nc),
        in_specs=[
            pl.BlockSpec((CHUNK, gw), lambda g, c: (c, g)),
            pl.BlockSpec((CHUNK, gw), lambda g, c: (c, xoff + g)),
            pl.BlockSpec((CHUNK, n), lambda g, c: (c, g)),
            pl.BlockSpec((CHUNK, n), lambda g, c: (c, ng + g)),
            pl.BlockSpec((1, CHUNK, LANES), lambda g, c: (g, c, 0)),
            pl.BlockSpec((1, hpg, CHUNK), lambda g, c: (g, 0, c)),
            pl.BlockSpec((CONV_WIDTH, gw), lambda g, c: (0, g)),
            pl.BlockSpec((CONV_WIDTH, n), lambda g, c: (0, boff + g)),
            pl.BlockSpec((CONV_WIDTH, n), lambda g, c: (0, coff + g)),
            pl.BlockSpec((1, gw), lambda g, c: (0, g)),
            pl.BlockSpec((1, n), lambda g, c: (0, boff + g)),
            pl.BlockSpec((1, n), lambda g, c: (0, coff + g)),
            head_spec, headt_spec, head_spec, headt_spec, head_spec,
            pl.BlockSpec((1, gw), lambda g, c: (0, g)),
        ],
        out_specs=pl.BlockSpec((CHUNK, gw), lambda g, c: (c, g)),
        out_shape=jax.ShapeDtypeStruct((s, D_INNER), BF16),
        scratch_shapes=[
            pltpu.VMEM((n, gw), F32),
            pltpu.VMEM((CHUNK, gw), BF16),
            pltpu.VMEM((CHUNK, gw), F32),
            pltpu.VMEM((8, gw), F32),
            pltpu.VMEM((8, n), F32),
            pltpu.VMEM((8, n), F32),
        ],
        compiler_params=_params(("parallel", "arbitrary")),
        name="ssd",
    )(zx, zx, bcdt, bcdt, dt_g, dt_gt, conv_w, conv_w, conv_w, cb2, cb2, cb2,
      bias, biast, alog, alogt, dsk, norm_g.reshape(1, -1))


SEL_TILE = 256
SEL_SHIFT = SEL_BLOCK.bit_length() - 1
WIN_TILES = WINDOW // Q_BLOCK + 1


def _selection_matrix(n_cmp_pad, n_sblk):
    ratio = SEL_BLOCK // CMP_STRIDE
    offs = np.arange(-(CMP_BLOCK // CMP_STRIDE), ratio + 1)
    ov = np.clip(np.minimum(offs * CMP_STRIDE + CMP_BLOCK, SEL_BLOCK) - np.maximum(offs * CMP_STRIDE, 0), 0, None)
    m = np.zeros((n_cmp_pad, n_sblk), np.float32)
    for o, w in zip(offs, ov):
        if w > 0:
            for j in range(n_sblk):
                c = j * ratio + o
                if 0 <= c < n_cmp_pad - 1:
                    m[c, j] = w / CMP_BLOCK
    return m


def _nsa_kernel(q_ref, gate_ref, kc_ref, vc_ref, ks_ref, vs_ref, kw_ref, vw_ref, selm_ref,
                o_ref,
                bias_ref, m_ref, l_ref, acc_ref, oc_ref):
    qb = pl.program_id(1)
    t0 = qb * Q_BLOCK
    R, QB, D = Q_PER_KV, Q_BLOCK, HEAD_DIM
    n_cmp = kc_ref.shape[1]
    n_sblk = selm_ref.shape[1]

    tok = t0 + lax.broadcasted_iota(jnp.int32, (QB, 1), 0)
    qs = jnp.concatenate([q_ref[:, r * D:(r + 1) * D] for r in range(R)], axis=0)

    cend = lax.broadcasted_iota(jnp.int32, (QB, n_cmp), 1) * CMP_STRIDE + (CMP_BLOCK - 1)
    mc = cend <= tok
    kc = kc_ref[0]
    vc = vc_ref[0]
    psum = jnp.zeros((QB, n_cmp), F32)
    for r in range(R):
        sc = _dot_nt(qs[r * QB:(r + 1) * QB], kc)
        sc = jnp.where(mc, sc, NEG)
        e = jnp.exp(sc - jnp.max(sc, axis=-1, keepdims=True))
        p = jnp.where(mc, e * (1.0 / jnp.sum(e, axis=-1, keepdims=True)), 0.0)
        psum = psum + p
        oc_ref[r] = _dot(p.astype(BF16), vc)

    selm = selm_ref[...].astype(BF16)
    imp = sum(_dot(piece, selm) for piece in _split3(psum))
    jj = lax.broadcasted_iota(jnp.int32, (QB, n_sblk), 1)
    jf = jj.astype(F32)
    cur = jnp.right_shift(tok, SEL_SHIFT)
    forced = (jj == 0) | (jj == cur) | (jj == cur - 1)
    score = jnp.where(jj <= cur, imp + jnp.where(forced, FORCE_BONUS, 0.0), NEG)
    sel = jnp.zeros((QB, n_sblk), F32)
    for _ in range(min(N_SELECT, n_sblk)):
        mx = jnp.max(score, axis=-1, keepdims=True)
        idx = jnp.min(jnp.where(score == mx, jf, float(n_sblk)), axis=-1, keepdims=True)
        hit = jf == idx
        sel = jnp.where(hit, 1.0, sel)
        score = jnp.where(hit, -jnp.inf, score)
    selb = sel.astype(BF16)

    n_tiles = (t0 + QB + SEL_TILE - 1) // SEL_TILE

    def bias_body(kt, c):
        k0 = kt * SEL_TILE
        kpos = k0 + lax.broadcasted_iota(jnp.int32, (n_sblk, SEL_TILE), 1)
        blk = lax.broadcasted_iota(jnp.int32, (n_sblk, SEL_TILE), 0)
        expand = jnp.where(jnp.right_shift(kpos, SEL_SHIFT) == blk, 1.0, 0.0).astype(BF16)
        picked = _dot(selb, expand)
        kq = k0 + lax.broadcasted_iota(jnp.int32, (QB, SEL_TILE), 1)
        bias_ref[kt] = jnp.where((picked > 0.5) & (kq <= tok), 0.0, NEG)
        return c

    lax.fori_loop(0, n_tiles, bias_body, 0)

    m_ref[...] = jnp.full(m_ref.shape, NEG, F32)
    l_ref[...] = jnp.zeros(l_ref.shape, F32)
    acc_ref[...] = jnp.zeros(acc_ref.shape, F32)

    def sel_body(kt, c):
        k0 = pl.multiple_of(kt * SEL_TILE, SEL_TILE)
        kt_k = ks_ref[pl.ds(k0, SEL_TILE), :]
        kt_v = vs_ref[pl.ds(k0, SEL_TILE), :]
        bias = bias_ref[kt]
        for r in range(R):
            rows = slice(r * QB, (r + 1) * QB)
            s = _dot_nt(qs[rows], kt_k) + bias
            m_old = m_ref[rows]
            m_new = jnp.maximum(m_old, jnp.max(s, axis=-1, keepdims=True))
            alpha = jnp.exp(m_old - m_new)
            p = jnp.exp(s - m_new)
            l_ref[rows] = alpha * l_ref[rows] + jnp.sum(p, axis=-1, keepdims=True)
            acc_ref[rows] = alpha * acc_ref[rows] + _dot(p.astype(BF16), kt_v)
            m_ref[rows] = m_new
        return c

    lax.fori_loop(0, n_tiles, sel_body, 0)

    kws, vws, bws = [], [], []
    for i in range(WIN_TILES):
        tile = qb - (WIN_TILES - 1) + i
        k0 = pl.multiple_of(jnp.maximum(tile, 0) * QB, QB)
        kws.append(kw_ref[pl.ds(k0, QB), :])
        vws.append(vw_ref[pl.ds(k0, QB), :])
        spos = tile * QB + lax.broadcasted_iota(jnp.int32, (QB, QB), 1)
        ok = (spos <= tok) & (spos > tok - WINDOW) & (spos >= 0)
        bws.append(jnp.where(ok, 0.0, NEG))
    kwin = jnp.concatenate(kws, axis=0)
    vwin = jnp.concatenate(vws, axis=0)
    bwin = jnp.concatenate(bws, axis=1)

    gates = gate_ref[...]
    for r in range(R):
        rows = slice(r * QB, (r + 1) * QB)
        sw = _dot_nt(qs[rows], kwin) + bwin
        e = jnp.exp(sw - jnp.max(sw, axis=-1, keepdims=True))
        ow = _dot(e.astype(BF16), vwin) * (1.0 / jnp.sum(e, axis=-1, keepdims=True))
        osel = acc_ref[rows] * (1.0 / l_ref[rows])
        o = (gates[:, 3 * r:3 * r + 1] * oc_ref[r] + gates[:, 3 * r + 1:3 * r + 2] * osel
             + gates[:, 3 * r + 2:3 * r + 3] * ow)
        o_ref[:, r * D:(r + 1) * D] = o.astype(o_ref.dtype)


def nsa_attention(q, gates, kc, vc, ks, vs, kw, vw):
    s = q.shape[0]
    nqb = s // Q_BLOCK
    n_cmp = kc.shape[1]
    n_sblk = s // SEL_BLOCK
    gw = Q_PER_KV * HEAD_DIM
    selm = jnp.asarray(_selection_matrix(n_cmp, n_sblk))
    kv_spec = pl.BlockSpec((s, HEAD_DIM), lambda g, i: (0, g))
    cmp_spec = pl.BlockSpec((1, n_cmp, HEAD_DIM), lambda g, i: (g, 0, 0))
    rows = Q_PER_KV * Q_BLOCK
    return pl.pallas_call(
        _nsa_kernel,
        grid=(KV_GROUPS, nqb),
        in_specs=[pl.BlockSpec((Q_BLOCK, gw), lambda g, i: (i, g)),
                  pl.BlockSpec((Q_BLOCK, LANES), lambda g, i: (i, g)),
                  cmp_spec, cmp_spec, kv_spec, kv_spec, kv_spec, kv_spec,
                  pl.BlockSpec((n_cmp, n_sblk), lambda g, i: (0, 0))],
        out_specs=pl.BlockSpec((Q_BLOCK, gw), lambda g, i: (i, g)),
        out_shape=jax.ShapeDtypeStruct((s, ATTN_HEADS * HEAD_DIM), BF16),
        scratch_shapes=[
            pltpu.VMEM((max(s // SEL_TILE, 1), Q_BLOCK, SEL_TILE), F32),
            pltpu.VMEM((rows, 1), F32),
            pltpu.VMEM((rows, 1), F32),
            pltpu.VMEM((rows, HEAD_DIM), F32),
            pltpu.VMEM((Q_PER_KV, Q_BLOCK, HEAD_DIM), F32),
        ],
        compiler_params=_params(("parallel", "arbitrary")),
        name="nsa_attention",
    )(q, gates, kc, vc, ks, vs, kw, vw, selm)


def _ffn(h, norm_g, w_in, w_out):
    act = ffn_in(rmsnorm(h, norm_g), w_in)
    return mm_res(act, w_out, h, 0.5)


def _mamba(h, norm_g, w_in, conv_w, conv_b, dt_bias, a_log, d_skip, ssm_norm, w_out):
    u = rmsnorm(h, norm_g)
    zx_cols = 2 * D_INNER
    zx = mm_plain(u, w_in, zx_cols, 512)
    w_tail = w_in[:, zx_cols:]
    w_tail = jnp.pad(w_tail, ((0, 0), (0, -w_tail.shape[1] % 256)))
    bcdt = mm_plain(u, w_tail, w_tail.shape[1], 256)
    y = ssd_mixer_core(zx, bcdt, conv_w, conv_b, dt_bias, a_log, d_skip, ssm_norm)
    return mm_res(y, w_out, h, 1.0)


def _shared_kv(h, kv_norm, kv_w, cmp_pe_k, cmp_w1_k, cmp_w2_k, cmp_pe_v, cmp_w1_v, cmp_w2_v,
               k_norm_cmp, k_norm_slc, k_norm_win):
    s = h.shape[0]
    gd = KV_GROUPS * HEAD_DIM
    kv = mm_plain(rmsnorm(h, kv_norm), kv_w, 6 * gd, 512)
    n16 = s // CMP_STRIDE

    def blocks16(t):
        return t.reshape(n16, CMP_STRIDE, KV_GROUPS, HEAD_DIM).transpose(2, 0, 1, 3).reshape(
            KV_GROUPS, n16, CMP_STRIDE * HEAD_DIM)

    cmp_tables = rope_tables(jnp.arange(n16) * CMP_STRIDE + (CMP_BLOCK - 1))
    pos_tables = rope_tables(jnp.arange(s))
    kc = compress(blocks16(kv[:, 0:gd]), cmp_pe_k, cmp_w1_k, cmp_w2_k, k_norm_cmp, cmp_tables, True)
    vc = compress(blocks16(kv[:, gd:2 * gd]), cmp_pe_v, cmp_w1_v, cmp_w2_v, k_norm_cmp, cmp_tables, False)
    ks = head_norm_rope(kv, 2 * gd, KV_GROUPS, k_norm_slc, pos_tables)
    kw = head_norm_rope(kv, 4 * gd, KV_GROUPS, k_norm_win, pos_tables)
    vs = kv[:, 3 * gd:4 * gd].astype(BF16)
    vw = kv[:, 5 * gd:6 * gd].astype(BF16)
    return kc, vc, ks, vs, kw, vw, pos_tables


def _nsa(h, norm_g, w_qg, q_norm_g, w_o, shared):
    kc, vc, ks, vs, kw, vw, pos_tables = shared
    u = rmsnorm(h, norm_g)
    qd = ATTN_HEADS * HEAD_DIM
    q = mm_plain(u, w_qg, qd, 512)
    per_g = Q_PER_KV * 3
    w_g = w_qg[:, qd:].reshape(D_MODEL, KV_GROUPS, per_g)
    w_g = jnp.pad(w_g, ((0, 0), (0, 0), (0, LANES - per_g))).reshape(D_MODEL, KV_GROUPS * LANES)
    gates = mm_plain(u, w_g, KV_GROUPS * LANES, KV_GROUPS * LANES, act="sigmoid")
    qn = head_norm_rope(q, 0, ATTN_HEADS, q_norm_g, pos_tables, scale=HEAD_DIM ** -0.5)
    o = nsa_attention(qn, gates, kc, vc, ks, vs, kw, vw)
    return mm_res(o, w_o, h, 1.0)


def kernel(x, ffn_a_norm, ffn_a_w_in, ffn_a_w_out, ffn_b_norm, ffn_b_w_in, ffn_b_w_out, mix_norm,
           ssm_w_in, ssm_conv_w, ssm_conv_b, ssm_dt_bias, ssm_a_log, ssm_d, ssm_norm, ssm_w_out,
           kv_norm, kv_w, cmp_pe_k, cmp_w1_k, cmp_w2_k, cmp_pe_v, cmp_w1_v, cmp_w2_v,
           k_norm_cmp, k_norm_slc, k_norm_win, attn_w_qg, attn_q_norm, attn_w_o):
    b, s, d = x.shape
    n_layers = ffn_a_norm.shape[0]
    n_a = ssm_w_in.shape[0]
    outs = []
    for bi in range(b):
        h = x.reshape(s, d) if b == 1 else x[bi]
        shared = None
        for i in range(n_layers):
            h = _ffn(h, ffn_a_norm[i], ffn_a_w_in[i], ffn_a_w_out[i])
            if i < n_a:
                h = _mamba(h, mix_norm[i], ssm_w_in[i], ssm_conv_w[i], ssm_conv_b[i], ssm_dt_bias[i],
                           ssm_a_log[i], ssm_d[i], ssm_norm[i], ssm_w_out[i])
            else:
                j = i - n_a
                h = _nsa(h, mix_norm[i], attn_w_qg[j], attn_q_norm[j], attn_w_o[j], shared)
            h = _ffn(h, ffn_b_norm[i], ffn_b_w_in[i], ffn_b_w_out[i])
            if i == n_a - 1:
                shared = _shared_kv(h, kv_norm, kv_w, cmp_pe_k, cmp_w1_k, cmp_w2_k, cmp_pe_v, cmp_w1_v,
                                    cmp_w2_v, k_norm_cmp, k_norm_slc, k_norm_win)
        outs.append(h)
    return jnp.stack(outs, axis=0)
```

```python
import functools
import math

import jax
import jax.numpy as jnp
import numpy as np
from jax import lax
from jax.experimental import pallas as pl
from jax.experimental.pallas import tpu as pltpu

F32 = jnp.float32
BF16 = jnp.bfloat16

D_MODEL = 4096
D_FF = 11008
D_INNER = 8192
SSM_HEAD_DIM = 64
SSM_GROUPS = 8
SSM_HPG = 16
SSM_STATE = 128
SSM_GROUP_W = SSM_HPG * SSM_HEAD_DIM
CONV_WIDTH = 4
CHUNK = 128
ATTN_HEADS = 32
KV_GROUPS = 4
Q_PER_KV = 8
HEAD_DIM = 128
CMP_BLOCK = 32
CMP_STRIDE = 16
SEL_BLOCK = 64
N_SELECT = 16
WINDOW = 512
Q_BLOCK = 128
ROPE_THETA = 500000.0
ROPE_DIM = 32
EPS = 1e-6
NEG = -1e30
FORCE_BONUS = 1e4

LANES = 128
VMEM_LIMIT = 56 * 1024 * 1024


def _params(sem):
    return pltpu.CompilerParams(dimension_semantics=sem, vmem_limit_bytes=VMEM_LIMIT)


def _dot(a, b):
    return jnp.dot(a, b, preferred_element_type=F32)


def _dot_nt(a, b):
    return lax.dot_general(a, b, (((1,), (1,)), ((), ())), preferred_element_type=F32)


def _dot_tn(a, b):
    return lax.dot_general(a, b, (((0,), (0,)), ((), ())), preferred_element_type=F32)


def _split3(x):
    hi = x.astype(BF16)
    r1 = x - hi.astype(F32)
    mid = r1.astype(BF16)
    lo = (r1 - mid.astype(F32)).astype(BF16)
    return hi, mid, lo


def _sigmoid(x):
    return 1.0 / (1.0 + jnp.exp(-x))


def _silu(x):
    return x * _sigmoid(x)


def _softplus(x):
    return jnp.maximum(x, 0.0) + jnp.log1p(jnp.exp(-jnp.abs(x)))


def _dot_cast(a_ref, w_ref, k_chunk):
    k = a_ref.shape[1]
    acc = None
    for k0 in range(0, k, k_chunk):
        part = _dot(a_ref[:, k0:k0 + k_chunk], w_ref[k0:k0 + k_chunk, :].astype(BF16))
        acc = part if acc is None else acc + part
    return acc


def _rmsnorm_kernel(x_ref, g_ref, o_ref):
    x = x_ref[...]
    ms = jnp.mean(x * x, axis=-1, keepdims=True)
    o_ref[...] = (x * lax.rsqrt(ms + EPS) * g_ref[...]).astype(o_ref.dtype)


def rmsnorm(x, g, tm=256):
    s, d = x.shape
    return pl.pallas_call(
        _rmsnorm_kernel,
        grid=(s // tm,),
        in_specs=[pl.BlockSpec((tm, d), lambda i: (i, 0)),
                  pl.BlockSpec((1, d), lambda i: (0, 0))],
        out_specs=pl.BlockSpec((tm, d), lambda i: (i, 0)),
        out_shape=jax.ShapeDtypeStruct((s, d), BF16),
        compiler_params=_params(("parallel",)),
        name="rmsnorm",
    )(x, g.reshape(1, d))


def _row_tile(s, want):
    return min(s, want)


def _w_spec(w, layer, k, tn, col0=0):
    if w.ndim == 2:
        return pl.BlockSpec((k, tn), lambda i, j: (0, j + col0))
    return pl.BlockSpec((None, k, tn), lambda i, j: (layer, 0, j + col0))


def _ffn_in_kernel(x_ref, wa_ref, wb_ref, o_ref):
    x = x_ref[...]
    a = _dot(x, wa_ref[...].astype(BF16))
    b = _dot(x, wb_ref[...].astype(BF16))
    o_ref[...] = (_silu(a) * b).astype(o_ref.dtype)


def ffn_in(xn, w_in, layer, tm=1024, tn=256):
    s, k = xn.shape
    f = w_in.shape[-1] // 2
    tm = _row_tile(s, tm)
    nb = f // tn
    return pl.pallas_call(
        _ffn_in_kernel,
        grid=(s // tm, nb),
        in_specs=[pl.BlockSpec((tm, k), lambda i, j: (i, 0)),
                  _w_spec(w_in, layer, k, tn),
                  _w_spec(w_in, layer, k, tn, nb)],
        out_specs=pl.BlockSpec((tm, tn), lambda i, j: (i, j)),
        out_shape=jax.ShapeDtypeStruct((s, f), BF16),
        compiler_params=_params(("parallel", "arbitrary")),
        name="ffn_in",
    )(xn, w_in, w_in)


def _k_chunk(k):
    for c in range(min(k, 5632), 0, -LANES):
        if c % LANES == 0 and k % c == 0:
            return c
    return k


def _mm_res_kernel(a_ref, w_ref, r_ref, o_ref, *, scale, k_chunk):
    acc = _dot_cast(a_ref, w_ref, k_chunk)
    o_ref[...] = r_ref[...] + scale * acc


def mm_res(a, w, layer, res, scale, tm=1024, tn=256):
    s, k = a.shape
    n = w.shape[-1]
    tm = _row_tile(s, tm)
    return pl.pallas_call(
        functools.partial(_mm_res_kernel, scale=scale, k_chunk=_k_chunk(k)),
        grid=(s // tm, n // tn),
        in_specs=[pl.BlockSpec((tm, k), lambda i, j: (i, 0), pipeline_mode=pl.Buffered(1)),
                  _w_spec(w, layer, k, tn),
                  pl.BlockSpec((tm, tn), lambda i, j: (i, j))],
        out_specs=pl.BlockSpec((tm, tn), lambda i, j: (i, j)),
        out_shape=jax.ShapeDtypeStruct((s, n), F32),
        compiler_params=_params(("parallel", "arbitrary")),
        name="mm_res",
    )(a, w, res)


def _mm_plain_kernel(a_ref, w_ref, o_ref, *, act):
    acc = _dot(a_ref[...], w_ref[...].astype(BF16))
    if act == "sigmoid":
        acc = _sigmoid(acc)
    o_ref[...] = acc.astype(o_ref.dtype)


def mm_plain(a, w, layer, n, tn, tm=1024, act=None, out_dtype=F32):
    s, k = a.shape
    tm = _row_tile(s, tm)
    return pl.pallas_call(
        functools.partial(_mm_plain_kernel, act=act),
        grid=(s // tm, n // tn),
        in_specs=[pl.BlockSpec((tm, k), lambda i, j: (i, 0)),
                  _w_spec(w, layer, k, tn)],
        out_specs=pl.BlockSpec((tm, tn), lambda i, j: (i, j)),
        out_shape=jax.ShapeDtypeStruct((s, n), out_dtype),
        compiler_params=_params(("parallel", "arbitrary")),
        name="mm_plain",
    )(a, w)


def rope_tables(pos):
    half = ROPE_DIM // 2
    inv = ROPE_THETA ** (-jnp.arange(half, dtype=F32) / half)
    ang = pos.astype(F32)[:, None] * inv[None, :]
    cos, sin = jnp.cos(ang), jnp.sin(ang)
    n = pos.shape[0]
    ones = jnp.ones((n, HEAD_DIM - ROPE_DIM), F32)
    zeros = jnp.zeros((n, HEAD_DIM - ROPE_DIM), F32)
    z16 = jnp.zeros((n, half), F32)
    cos_t = jnp.concatenate([cos, cos, ones], axis=1)
    sin_a = jnp.concatenate([z16, sin, zeros], axis=1)
    sin_b = jnp.concatenate([-sin, z16, zeros], axis=1)
    return cos_t, sin_a, sin_b


def _norm_rope(x, g, cos_t, sin_a, sin_b):
    ms = jnp.mean(x * x, axis=-1, keepdims=True)
    y = x * lax.rsqrt(ms + EPS) * g
    half = ROPE_DIM // 2
    return (y * cos_t + pltpu.roll(y, half, 1) * sin_a
            + pltpu.roll(y, HEAD_DIM - half, 1) * sin_b)


def _head_norm_rope_kernel(x_ref, g_ref, c_ref, sa_ref, sb_ref, o_ref, *, nh, scale):
    g = g_ref[...]
    c, sa, sb = c_ref[...], sa_ref[...], sb_ref[...]
    for h in range(nh):
        x = x_ref[:, h * HEAD_DIM:(h + 1) * HEAD_DIM]
        y = _norm_rope(x, g, c, sa, sb)
        o_ref[:, h * HEAD_DIM:(h + 1) * HEAD_DIM] = (y * scale).astype(o_ref.dtype)


def head_norm_rope(x, col0, nh, g, tables, scale=1.0, tm=256):
    s = x.shape[0]
    w = nh * HEAD_DIM
    cb = col0 // w
    tm = _row_tile(s, tm)
    tab_spec = pl.BlockSpec((tm, HEAD_DIM), lambda i: (i, 0))
    return pl.pallas_call(
        functools.partial(_head_norm_rope_kernel, nh=nh, scale=scale),
        grid=(s // tm,),
        in_specs=[pl.BlockSpec((tm, w), lambda i: (i, cb)),
                  pl.BlockSpec((1, HEAD_DIM), lambda i: (0, 0)),
                  tab_spec, tab_spec, tab_spec],
        out_specs=pl.BlockSpec((tm, w), lambda i: (i, 0)),
        out_shape=jax.ShapeDtypeStruct((s, w), BF16),
        compiler_params=_params(("parallel",)),
        name="head_norm_rope",
    )(x, g.reshape(1, HEAD_DIM), *tables)


def _compress_kernel(t_ref, pe_ref, w1_ref, w2_ref, g_ref, c_ref, sa_ref, sb_ref, o_ref, *, is_key):
    half = CMP_STRIDE * HEAD_DIM
    t = t_ref[0]
    top = (t + pe_ref[:, :half]).astype(BF16)
    bot = (t + pe_ref[:, half:]).astype(BF16)
    p = _dot(top, w1_ref[:half, :].astype(BF16))
    q = _dot(bot, w1_ref[half:, :].astype(BF16))
    n = t.shape[0]
    hid = p + pltpu.roll(q, n - 1, 0)
    out = _dot(_silu(hid).astype(BF16), w2_ref[...].astype(BF16))
    if is_key:
        out = _norm_rope(out, g_ref[...], c_ref[...], sa_ref[...], sb_ref[...])
    o_ref[0] = out.astype(o_ref.dtype)


def compress(t16, pe, w1, w2, g, tables, is_key):
    ng, n, wid = t16.shape
    tab_spec = pl.BlockSpec((n, HEAD_DIM), lambda i: (0, 0))
    return pl.pallas_call(
        functools.partial(_compress_kernel, is_key=is_key),
        grid=(ng,),
        in_specs=[pl.BlockSpec((1, n, wid), lambda i: (i, 0, 0)),
                  pl.BlockSpec((1, 2 * wid), lambda i: (0, 0)),
                  pl.BlockSpec((2 * wid, HEAD_DIM), lambda i: (0, 0)),
                  pl.BlockSpec((HEAD_DIM, HEAD_DIM), lambda i: (0, 0)),
                  pl.BlockSpec((1, HEAD_DIM), lambda i: (0, 0)),
                  tab_spec, tab_spec, tab_spec],
        out_specs=pl.BlockSpec((1, n, HEAD_DIM), lambda i: (i, 0, 0)),
        out_shape=jax.ShapeDtypeStruct((ng, n, HEAD_DIM), BF16),
        compiler_params=_params(("parallel",)),
        name="compress_k" if is_key else "compress_v",
    )(t16, pe.reshape(1, 2 * wid), w1, w2, g.reshape(1, HEAD_DIM), *tables)


def _conv_silu(x, carry_ref, w_ref, b_ref, first):
    @pl.when(first)
    def _():
        carry_ref[...] = jnp.zeros_like(carry_ref)

    full = jnp.concatenate([carry_ref[...], x], axis=0)
    n = x.shape[0]
    acc = b_ref[...] + w_ref[CONV_WIDTH - 1:CONV_WIDTH, :] * x
    for k in range(1, CONV_WIDTH):
        acc = acc + w_ref[CONV_WIDTH - 1 - k:CONV_WIDTH - k, :] * full[8 - k:8 - k + n]
    carry_ref[...] = x[n - 8:, :]
    return _silu(acc)


def _ssd_kernel(z_ref, x_ref, b_ref, c_ref, dt_ref, dtt_ref,
                cwx_ref, cwb_ref, cwc_ref, cbx_ref, cbb_ref, cbc_ref,
                bias_ref, biast_ref, alog_ref, alogt_ref, d_ref, ng_ref,
                o_ref,
                state_ref, xs_ref, y_ref, cx_ref, cb_ref, cc_ref):
    ci = pl.program_id(1)
    first = ci == 0
    L = CHUNK

    @pl.when(first)
    def _():
        state_ref[...] = jnp.zeros_like(state_ref)

    xg = _conv_silu(x_ref[...], cx_ref, cwx_ref, cbx_ref, first)
    bm = _conv_silu(b_ref[...], cb_ref, cwb_ref, cbb_ref, first)
    cm = _conv_silu(c_ref[...], cc_ref, cwc_ref, cbc_ref, first)

    dt = _softplus(dt_ref[0] + bias_ref[0])
    dtt = _softplus(dtt_ref[0] + biast_ref[0])
    a = dt * (-jnp.exp(alog_ref[0]))
    at = dtt * (-jnp.exp(alogt_ref[0]))

    row = lax.broadcasted_iota(jnp.int32, (L, L), 0)
    col = lax.broadcasted_iota(jnp.int32, (L, L), 1)
    causal = row >= col
    tril = causal.astype(BF16)
    triu = (row <= col).astype(BF16)
    cum = sum(_dot(tril, p) for p in _split3(a))
    cumt = sum(_dot(p, triu) for p in _split3(at))
    ecum = jnp.exp(cum)
    last = cum[L - 1:L, :]
    toend = jnp.exp(last - cum)
    elast = jnp.exp(last)

    bmb = bm.astype(BF16)
    cmb = cm.astype(BF16)
    cbmat = _dot_nt(cmb, bmb)
    yoff = _dot(cmb, state_ref[...].astype(BF16))

    lane = lax.broadcasted_iota(jnp.int32, (L, LANES), 1)
    lo = lane < SSM_HEAD_DIM
    lane1 = lax.broadcasted_iota(jnp.int32, (1, LANES), 1)
    lo1 = lane1 < SSM_HEAD_DIM
    dsk = d_ref[0]
    for k in range(SSM_HPG // 2):
        h0, h1 = 2 * k, 2 * k + 1
        sl = slice(k * LANES, (k + 1) * LANES)
        xp = xg[:, sl]
        xdt = xp * jnp.where(lo, dt[:, h0:h0 + 1], dt[:, h1:h1 + 1])
        g0 = (cbmat * jnp.exp(jnp.where(causal, cum[:, h0:h0 + 1] - cumt[h0:h0 + 1, :], -jnp.inf))).astype(BF16)
        g1 = (cbmat * jnp.exp(jnp.where(causal, cum[:, h1:h1 + 1] - cumt[h1:h1 + 1, :], -jnp.inf))).astype(BF16)
        ydiag = (_dot(g0, jnp.where(lo, xdt, 0.0).astype(BF16))
                 + _dot(g1, jnp.where(lo, 0.0, xdt).astype(BF16)))
        y = ydiag + yoff[:, sl] * jnp.where(lo, ecum[:, h0:h0 + 1], ecum[:, h1:h1 + 1])
        y = y + jnp.where(lo1, dsk[:, h0:h0 + 1], dsk[:, h1:h1 + 1]) * xp
        y_ref[:, sl] = y * _silu(z_ref[:, sl])
        xs_ref[:, sl] = (xdt * jnp.where(lo, toend[:, h0:h0 + 1], toend[:, h1:h1 + 1])).astype(BF16)
        state_ref[:, sl] = state_ref[:, sl] * jnp.where(lo1, elast[:, h0:h0 + 1], elast[:, h1:h1 + 1])

    state_ref[...] += _dot_tn(bmb, xs_ref[...])

    y = y_ref[...]
    ms = jnp.mean(y * y, axis=-1, keepdims=True)
    o_ref[...] = (y * lax.rsqrt(ms + EPS) * ng_ref[...]).astype(o_ref.dtype)


def ssd_mixer_core(zx, bcdt, conv_w, conv_b, dt_bias, a_log, d_skip, norm_g):
    s = zx.shape[0]
    nc = s // CHUNK
    gw, n, hpg, ng = SSM_GROUP_W, SSM_STATE, SSM_HPG, SSM_GROUPS
    dt_raw = bcdt[:, 2 * ng * n:2 * ng * n + ng * hpg]
    pad = ((0, 0), (0, 0), (0, LANES - hpg))
    dt_g = jnp.pad(dt_raw.reshape(s, ng, hpg).transpose(1, 0, 2), pad)
    dt_gt = dt_raw.reshape(s, ng, hpg).transpose(1, 2, 0)

    def per_head(v):
        return jnp.pad(v.reshape(ng, 1, hpg), pad), v.reshape(ng, hpg, 1)

    bias, biast = per_head(dt_bias)
    alog, alogt = per_head(a_log)
    dsk, _ = per_head(d_skip)
    cb2 = conv_b.reshape(1, -1)
    xoff, boff, coff = D_INNER // gw, D_INNER // n, D_INNER // n + ng
    head_spec = pl.BlockSpec((1, 1, LANES), lambda g, c: (g, 0, 0))
    headt_spec = pl.BlockSpec((1, hpg, 1), lambda g, c: (g, 0, 0))
    return pl.pallas_call(
        _ssd_kernel,
        grid=(ng, nc),
        in_specs=[
            pl.BlockSpec((CHUNK, gw), lambda g, c: (c, g)),
            pl.BlockSpec((CHUNK, gw), lambda g, c: (c, xoff + g)),
            pl.BlockSpec((CHUNK, n), lambda g, c: (c, g)),
            pl.BlockSpec((CHUNK, n), lambda g, c: (c, ng + g)),
            pl.BlockSpec((1, CHUNK, LANES), lambda g, c: (g, c, 0)),
            pl.BlockSpec((1, hpg, CHUNK), lambda g, c: (g, 0, c)),
            pl.BlockSpec((CONV_WIDTH, gw), lambda g, c: (0, g)),
            pl.BlockSpec((CONV_WIDTH, n), lambda g, c: (0, boff + g)),
            pl.BlockSpec((CONV_WIDTH, n), lambda g, c: (0, coff + g)),
            pl.BlockSpec((1, gw), lambda g, c: (0, g)),
            pl.BlockSpec((1, n), lambda g, c: (0, boff + g)),
            pl.BlockSpec((1, n), lambda g, c: (0, coff + g)),
            head_spec, headt_spec, head_spec, headt_spec, head_spec,
            pl.BlockSpec((1, gw), lambda g, c: (0, g)),
        ],
        out_specs=pl.BlockSpec((CHUNK, gw), lambda g, c: (c, g)),
        out_shape=jax.ShapeDtypeStruct((s, D_INNER), BF16),
        scratch_shapes=[
            pltpu.VMEM((n, gw), F32),
            pltpu.VMEM((CHUNK, gw), BF16),
            pltpu.VMEM((CHUNK, gw), F32),
            pltpu.VMEM((8, gw), F32),
            pltpu.VMEM((8, n), F32),
            pltpu.VMEM((8, n), F32),
        ],
        compiler_params=_params(("parallel", "arbitrary")),
        name="ssd",
    )(zx, zx, bcdt, bcdt, dt_g, dt_gt, conv_w, conv_w, conv_w, cb2, cb2, cb2,
      bias, biast, alog, alogt, dsk, norm_g.reshape(1, -1))


SEL_TILE = 512
SEL_SHIFT = SEL_BLOCK.bit_length() - 1
WIN_TILES = WINDOW // Q_BLOCK + 1
WIN_KEYS = WIN_TILES * Q_BLOCK


def _selection_matrix(n_cmp_pad, n_sblk):
    ratio = SEL_BLOCK // CMP_STRIDE
    offs = np.arange(-(CMP_BLOCK // CMP_STRIDE), ratio + 1)
    ov = np.clip(np.minimum(offs * CMP_STRIDE + CMP_BLOCK, SEL_BLOCK) - np.maximum(offs * CMP_STRIDE, 0), 0, None)
    m = np.zeros((n_cmp_pad, n_sblk), np.float32)
    for o, w in zip(offs, ov):
        if w > 0:
            for j in range(n_sblk):
                c = j * ratio + o
                if 0 <= c < n_cmp_pad - 1:
                    m[c, j] = w / CMP_BLOCK
    return m


def _rows(r):
    return slice(r * Q_BLOCK, (r + 1) * Q_BLOCK)


def _lane_fold(x, op):
    out = x[:, :LANES]
    for f in range(1, x.shape[1] // LANES):
        out = op(out, x[:, f * LANES:(f + 1) * LANES])
    return out


def _tile_lanes(x, width):
    return jnp.concatenate([x] * (width // LANES), axis=1)


def _row_stat(x, op, reduce):
    return jnp.broadcast_to(reduce(_lane_fold(x, op), axis=-1, keepdims=True), (x.shape[0], LANES))


def _nsa_kernel(q_ref, gate_ref, kc_ref, vc_ref, ks_ref, vs_ref, kw_ref, vw_ref, selmt_ref,
                o_ref,
                bias_ref, q_scr, s_ref, p_ref, b2_ref, psum_ref, stm_ref, stl_ref, wl_ref,
                mx_ref, lacc_ref, acc_ref, oc_ref, ow_ref):
    qb = pl.program_id(1)
    t0 = qb * Q_BLOCK
    R, QB, D = Q_PER_KV, Q_BLOCK, HEAD_DIM
    n_cmp = kc_ref.shape[1]
    n_sblk = selmt_ref.shape[0]

    tok = t0 + lax.broadcasted_iota(jnp.int32, (QB, 1), 0)
    for r in range(R):
        q_scr[_rows(r), :] = q_ref[:, r * D:(r + 1) * D]

    cend = lax.broadcasted_iota(jnp.int32, (QB, n_cmp), 1) * CMP_STRIDE + (CMP_BLOCK - 1)
    b2_ref[:, :n_cmp] = jnp.where(cend <= tok, 0.0, NEG)
    s_ref[:, :n_cmp] = _dot_nt(q_scr[...], kc_ref[0])
    for r in range(R):
        stm_ref[_rows(r)] = _row_stat(s_ref[_rows(r), :n_cmp] + b2_ref[:, :n_cmp], jnp.maximum, jnp.max)
    for r in range(R):
        e = jnp.exp(s_ref[_rows(r), :n_cmp] + b2_ref[:, :n_cmp] - _tile_lanes(stm_ref[_rows(r)], n_cmp))
        s_ref[_rows(r), :n_cmp] = e
        stl_ref[_rows(r)] = _row_stat(e, jnp.add, jnp.sum)
    has_key = tok >= CMP_BLOCK - 1
    psum_ref[...] = jnp.zeros(psum_ref.shape, F32)
    for r in range(R):
        inv = jnp.where(has_key, 1.0 / stl_ref[_rows(r)], 0.0)
        p = s_ref[_rows(r), :n_cmp] * _tile_lanes(inv, n_cmp)
        psum_ref[...] += p
        p_ref[_rows(r), :n_cmp] = p.astype(BF16)
    oc_ref[...] = _dot(p_ref[:, :n_cmp], vc_ref[0])
    psum = psum_ref[...]

    kws, vws = [], []
    for i in range(WIN_TILES):
        tile = qb - (WIN_TILES - 1) + i
        k0 = pl.multiple_of(jnp.maximum(tile, 0) * QB, QB)
        kws.append(kw_ref[pl.ds(k0, QB), :])
        vws.append(vw_ref[pl.ds(k0, QB), :])
        spos = tile * QB + lax.broadcasted_iota(jnp.int32, (QB, QB), 1)
        ok = (spos <= tok) & (spos > tok - WINDOW) & (spos >= 0)
        b2_ref[:, i * QB:(i + 1) * QB] = jnp.where(ok, 0.0, NEG)
    s_ref[:, :WIN_KEYS] = _dot_nt(q_scr[...], jnp.concatenate(kws, axis=0))
    for r in range(R):
        stm_ref[_rows(r)] = _row_stat(s_ref[_rows(r), :WIN_KEYS] + b2_ref[:, :WIN_KEYS], jnp.maximum, jnp.max)
    for r in range(R):
        e = jnp.exp(s_ref[_rows(r), :WIN_KEYS] + b2_ref[:, :WIN_KEYS] - _tile_lanes(stm_ref[_rows(r)], WIN_KEYS))
        p_ref[_rows(r), :WIN_KEYS] = e.astype(BF16)
        wl_ref[_rows(r)] = _row_stat(e, jnp.add, jnp.sum)
    ow_ref[...] = _dot(p_ref[:, :WIN_KEYS], jnp.concatenate(vws, axis=0))

    selmt = selmt_ref[...].astype(BF16)
    imp = sum(_dot_nt(selmt, piece) for piece in _split3(psum))
    jj = lax.broadcasted_iota(jnp.int32, (n_sblk, QB), 0)
    jf = jj.astype(F32)
    tok_t = t0 + lax.broadcasted_iota(jnp.int32, (1, QB), 1)
    cur = jnp.right_shift(tok_t, SEL_SHIFT)
    forced = (jj == 0) | (jj == cur) | (jj == cur - 1)
    score = jnp.where(jj <= cur, imp + jnp.where(forced, FORCE_BONUS, 0.0), NEG)
    sel_t = jnp.zeros((n_sblk, QB), F32)
    for _ in range(min(N_SELECT, n_sblk)):
        mx = jnp.max(score, axis=0, keepdims=True)
        idx = jnp.min(jnp.where(score == mx, jf, float(n_sblk)), axis=0, keepdims=True)
        hit = jf == idx
        sel_t = jnp.where(hit, 1.0, sel_t)
        score = jnp.where(hit, -jnp.inf, score)
    selb = sel_t.T.astype(BF16)

    n_tiles = (t0 + QB + SEL_TILE - 1) // SEL_TILE
    mx_ref[...] = jnp.full(mx_ref.shape, NEG, F32)
    lacc_ref[...] = jnp.zeros(lacc_ref.shape, F32)
    acc_ref[...] = jnp.zeros(acc_ref.shape, F32)

    def max_body(kt, c):
        k0 = pl.multiple_of(kt * SEL_TILE, SEL_TILE)
        kpos = k0 + lax.broadcasted_iota(jnp.int32, (n_sblk, SEL_TILE), 1)
        blk = lax.broadcasted_iota(jnp.int32, (n_sblk, SEL_TILE), 0)
        expand = jnp.where(jnp.right_shift(kpos, SEL_SHIFT) == blk, 1.0, 0.0).astype(BF16)
        picked = _dot(selb, expand)
        kq = k0 + lax.broadcasted_iota(jnp.int32, (QB, SEL_TILE), 1)
        bias_ref[kt] = jnp.where((picked > 0.5) & (kq <= tok), 0.0, NEG)
        s_ref[:, :SEL_TILE] = _dot_nt(q_scr[...], ks_ref[pl.ds(k0, SEL_TILE), :])
        for r in range(R):
            m = _lane_fold(s_ref[_rows(r), :SEL_TILE] + bias_ref[kt], jnp.maximum)
            mx_ref[_rows(r)] = jnp.maximum(mx_ref[_rows(r)], m)
        return c

    lax.fori_loop(0, n_tiles, max_body, 0)

    for r in range(R):
        mx_ref[_rows(r)] = jnp.broadcast_to(jnp.max(mx_ref[_rows(r)], axis=-1, keepdims=True), (QB, LANES))

    def sum_body(kt, c):
        k0 = pl.multiple_of(kt * SEL_TILE, SEL_TILE)
        s_ref[:, :SEL_TILE] = _dot_nt(q_scr[...], ks_ref[pl.ds(k0, SEL_TILE), :])
        for r in range(R):
            e = jnp.exp(s_ref[_rows(r), :SEL_TILE] + bias_ref[kt] - _tile_lanes(mx_ref[_rows(r)], SEL_TILE))
            lacc_ref[_rows(r)] += _lane_fold(e, jnp.add)
            p_ref[_rows(r), :SEL_TILE] = e.astype(BF16)
        acc_ref[...] += _dot(p_ref[:, :SEL_TILE], vs_ref[pl.ds(k0, SEL_TILE), :])
        return c

    lax.fori_loop(0, n_tiles, sum_body, 0)

    gates = gate_ref[...]
    for r in range(R):
        osel = acc_ref[_rows(r)] * (1.0 / jnp.sum(lacc_ref[_rows(r)], axis=-1, keepdims=True))
        ow = ow_ref[_rows(r)] * (1.0 / wl_ref[_rows(r)])
        o = (gates[:, 3 * r:3 * r + 1] * oc_ref[_rows(r)] + gates[:, 3 * r + 1:3 * r + 2] * osel
             + gates[:, 3 * r + 2:3 * r + 3] * ow)
        o_ref[:, r * D:(r + 1) * D] = o.astype(o_ref.dtype)


def nsa_attention(q, gates, kc, vc, ks, vs, kw, vw):
    s = q.shape[0]
    nqb = s // Q_BLOCK
    n_cmp = kc.shape[1]
    n_sblk = s // SEL_BLOCK
    gw = Q_PER_KV * HEAD_DIM
    selmt = jnp.asarray(_selection_matrix(n_cmp, n_sblk).T)
    kv_spec = pl.BlockSpec((s, HEAD_DIM), lambda g, i: (0, g))
    cmp_spec = pl.BlockSpec((1, n_cmp, HEAD_DIM), lambda g, i: (g, 0, 0))
    rows = Q_PER_KV * Q_BLOCK
    wide = max(WIN_KEYS, n_cmp, SEL_TILE)
    return pl.pallas_call(
        _nsa_kernel,
        grid=(KV_GROUPS, nqb),
        in_specs=[pl.BlockSpec((Q_BLOCK, gw), lambda g, i: (i, g)),
                  pl.BlockSpec((Q_BLOCK, LANES), lambda g, i: (i, g)),
                  cmp_spec, cmp_spec, kv_spec, kv_spec, kv_spec, kv_spec,
                  pl.BlockSpec((n_sblk, n_cmp), lambda g, i: (0, 0))],
        out_specs=pl.BlockSpec((Q_BLOCK, gw), lambda g, i: (i, g)),
        out_shape=jax.ShapeDtypeStruct((s, ATTN_HEADS * HEAD_DIM), BF16),
        scratch_shapes=[
            pltpu.VMEM((max(s // SEL_TILE, 1), Q_BLOCK, SEL_TILE), F32),
            pltpu.VMEM((rows, HEAD_DIM), BF16),
            pltpu.VMEM((rows, wide), F32),
            pltpu.VMEM((rows, wide), BF16),
            pltpu.VMEM((Q_BLOCK, wide), F32),
            pltpu.VMEM((Q_BLOCK, n_cmp), F32),
            pltpu.VMEM((rows, LANES), F32),
            pltpu.VMEM((rows, LANES), F32),
            pltpu.VMEM((rows, LANES), F32),
            pltpu.VMEM((rows, LANES), F32),
            pltpu.VMEM((rows, LANES), F32),
            pltpu.VMEM((rows, HEAD_DIM), F32),
            pltpu.VMEM((rows, HEAD_DIM), F32),
            pltpu.VMEM((rows, HEAD_DIM), F32),
        ],
        compiler_params=_params(("parallel", "arbitrary")),
        name="nsa_attention",
    )(q, gates, kc, vc, ks, vs, kw, vw, selmt)


def _ffn(h, norm_g, w_in, w_out, layer):
    act = ffn_in(rmsnorm(h, norm_g[layer]), w_in, layer)
    return mm_res(act, w_out, layer, h, 0.5)


def _mamba(h, norm_g, w_in, conv_w, conv_b, dt_bias, a_log, d_skip, ssm_norm, w_out, layer):
    u = rmsnorm(h, norm_g)
    zx_cols = 2 * D_INNER
    zx = mm_plain(u, w_in, layer, zx_cols, 512)
    w_tail = w_in[layer, :, zx_cols:]
    w_tail = jnp.pad(w_tail, ((0, 0), (0, -w_tail.shape[1] % 256)))
    bcdt = mm_plain(u, w_tail, 0, w_tail.shape[1], 256)
    y = ssd_mixer_core(zx, bcdt, conv_w[layer], conv_b[layer], dt_bias[layer], a_log[layer], d_skip[layer],
                       ssm_norm[layer])
    return mm_res(y, w_out, layer, h, 1.0)


def _shared_kv(h, kv_norm, kv_w, cmp_pe_k, cmp_w1_k, cmp_w2_k, cmp_pe_v, cmp_w1_v, cmp_w2_v,
               k_norm_cmp, k_norm_slc, k_norm_win):
    s = h.shape[0]
    gd = KV_GROUPS * HEAD_DIM
    kv = mm_plain(rmsnorm(h, kv_norm), kv_w, 0, 6 * gd, 512)
    n16 = s // CMP_STRIDE

    def blocks16(t):
        return t.reshape(n16, CMP_STRIDE, KV_GROUPS, HEAD_DIM).transpose(2, 0, 1, 3).reshape(
            KV_GROUPS, n16, CMP_STRIDE * HEAD_DIM)

    cmp_tables = rope_tables(jnp.arange(n16) * CMP_STRIDE + (CMP_BLOCK - 1))
    pos_tables = rope_tables(jnp.arange(s))
    kc = compress(blocks16(kv[:, 0:gd]), cmp_pe_k, cmp_w1_k, cmp_w2_k, k_norm_cmp, cmp_tables, True)
    vc = compress(blocks16(kv[:, gd:2 * gd]), cmp_pe_v, cmp_w1_v, cmp_w2_v, k_norm_cmp, cmp_tables, False)
    ks = head_norm_rope(kv, 2 * gd, KV_GROUPS, k_norm_slc, pos_tables)
    kw = head_norm_rope(kv, 4 * gd, KV_GROUPS, k_norm_win, pos_tables)
    vs = kv[:, 3 * gd:4 * gd].astype(BF16)
    vw = kv[:, 5 * gd:6 * gd].astype(BF16)
    return kc, vc, ks, vs, kw, vw, pos_tables


def _nsa(h, norm_g, w_qg, q_norm_g, w_o, shared, layer):
    kc, vc, ks, vs, kw, vw, pos_tables = shared
    u = rmsnorm(h, norm_g)
    qd = ATTN_HEADS * HEAD_DIM
    q = mm_plain(u, w_qg, layer, qd, 512)
    per_g = Q_PER_KV * 3
    w_g = w_qg[layer, :, qd:].reshape(D_MODEL, KV_GROUPS, per_g)
    w_g = jnp.pad(w_g, ((0, 0), (0, 0), (0, LANES - per_g))).reshape(D_MODEL, KV_GROUPS * LANES)
    gates = mm_plain(u, w_g, 0, KV_GROUPS * LANES, KV_GROUPS * LANES, act="sigmoid")
    qn = head_norm_rope(q, 0, ATTN_HEADS, q_norm_g[layer], pos_tables, scale=HEAD_DIM ** -0.5)
    o = nsa_attention(qn, gates, kc, vc, ks, vs, kw, vw)
    return mm_res(o, w_o, layer, h, 1.0)


def kernel(x, ffn_a_norm, ffn_a_w_in, ffn_a_w_out, ffn_b_norm, ffn_b_w_in, ffn_b_w_out, mix_norm,
           ssm_w_in, ssm_conv_w, ssm_conv_b, ssm_dt_bias, ssm_a_log, ssm_d, ssm_norm, ssm_w_out,
           kv_norm, kv_w, cmp_pe_k, cmp_w1_k, cmp_w2_k, cmp_pe_v, cmp_w1_v, cmp_w2_v,
           k_norm_cmp, k_norm_slc, k_norm_win, attn_w_qg, attn_q_norm, attn_w_o):
    b, s, d = x.shape
    n_layers = ffn_a_norm.shape[0]
    n_a = ssm_w_in.shape[0]
    outs = []
    for bi in range(b):
        h = x.reshape(s, d) if b == 1 else x[bi]
        shared = None
        for i in range(n_layers):
            h = _ffn(h, ffn_a_norm, ffn_a_w_in, ffn_a_w_out, i)
            if i < n_a:
                h = _mamba(h, mix_norm[i], ssm_w_in, ssm_conv_w, ssm_conv_b, ssm_dt_bias,
                           ssm_a_log, ssm_d, ssm_norm, ssm_w_out, i)
            else:
                h = _nsa(h, mix_norm[i], attn_w_qg, attn_q_norm, attn_w_o, shared, i - n_a)
            h = _ffn(h, ffn_b_norm, ffn_b_w_in, ffn_b_w_out, i)
            if i == n_a - 1:
                shared = _shared_kv(h, kv_norm, kv_w, cmp_pe_k, cmp_w1_k, cmp_w2_k, cmp_pe_v, cmp_w1_v,
                                    cmp_w2_v, k_norm_cmp, k_norm_slc, k_norm_win)
        outs.append(h)
    return jnp.stack(outs, axis=0)
```

```python
import functools
import math

import jax
import jax.numpy as jnp
import numpy as np
from jax import lax
from jax.experimental import pallas as pl
from jax.experimental.pallas import tpu as pltpu

F32 = jnp.float32
BF16 = jnp.bfloat16

D_MODEL = 4096
D_FF = 11008
D_INNER = 8192
SSM_HEAD_DIM = 64
SSM_GROUPS = 8
SSM_HPG = 16
SSM_STATE = 128
SSM_GROUP_W = SSM_HPG * SSM_HEAD_DIM
CONV_WIDTH = 4
CHUNK = 128
ATTN_HEADS = 32
KV_GROUPS = 4
Q_PER_KV = 8
HEAD_DIM = 128
CMP_BLOCK = 32
CMP_STRIDE = 16
SEL_BLOCK = 64
N_SELECT = 16
WINDOW = 512
Q_BLOCK = 128
ROPE_THETA = 500000.0
ROPE_DIM = 32
EPS = 1e-6
NEG = -1e30
FORCE_BONUS = 1e4

LANES = 128
VMEM_LIMIT = 56 * 1024 * 1024


def _params(sem):
    return pltpu.CompilerParams(dimension_semantics=sem, vmem_limit_bytes=VMEM_LIMIT)


def _dot(a, b):
    return jnp.dot(a, b, preferred_element_type=F32)


def _dot_nt(a, b):
    return lax.dot_general(a, b, (((1,), (1,)), ((), ())), preferred_element_type=F32)


def _dot_tn(a, b):
    return lax.dot_general(a, b, (((0,), (0,)), ((), ())), preferred_element_type=F32)


def _split3(x):
    hi = x.astype(BF16)
    r1 = x - hi.astype(F32)
    mid = r1.astype(BF16)
    lo = (r1 - mid.astype(F32)).astype(BF16)
    return hi, mid, lo


def _sigmoid(x):
    return 1.0 / (1.0 + jnp.exp(-x))


def _silu(x):
    return x * _sigmoid(x)


def _softplus(x):
    return jnp.maximum(x, 0.0) + jnp.log1p(jnp.exp(-jnp.abs(x)))


def _dot_cast(a_ref, w_ref, k_chunk):
    k = a_ref.shape[1]
    acc = None
    for k0 in range(0, k, k_chunk):
        part = _dot(a_ref[:, k0:k0 + k_chunk], w_ref[k0:k0 + k_chunk, :].astype(BF16))
        acc = part if acc is None else acc + part
    return acc


def _rmsnorm_kernel(x_ref, g_ref, o_ref):
    x = x_ref[...]
    ms = jnp.mean(x * x, axis=-1, keepdims=True)
    o_ref[...] = (x * lax.rsqrt(ms + EPS) * g_ref[...]).astype(o_ref.dtype)


def rmsnorm(x, g, tm=256):
    s, d = x.shape
    return pl.pallas_call(
        _rmsnorm_kernel,
        grid=(s // tm,),
        in_specs=[pl.BlockSpec((tm, d), lambda i: (i, 0)),
                  pl.BlockSpec((1, d), lambda i: (0, 0))],
        out_specs=pl.BlockSpec((tm, d), lambda i: (i, 0)),
        out_shape=jax.ShapeDtypeStruct((s, d), BF16),
        compiler_params=_params(("parallel",)),
        name="rmsnorm",
    )(x, g.reshape(1, d))


def _row_tile(s, want):
    return min(s, want)


def _w_spec(w, layer, k, tn, col0=0):
    if w.ndim == 2:
        return pl.BlockSpec((k, tn), lambda i, j: (0, j + col0))
    return pl.BlockSpec((None, k, tn), lambda i, j: (layer, 0, j + col0))


def _ffn_in_kernel(x_ref, wa_ref, wb_ref, o_ref):
    x = x_ref[...]
    a = _dot(x, wa_ref[...].astype(BF16))
    b = _dot(x, wb_ref[...].astype(BF16))
    o_ref[...] = (_silu(a) * b).astype(o_ref.dtype)


def ffn_in(xn, w_in, layer, tm=1024, tn=256):
    s, k = xn.shape
    f = w_in.shape[-1] // 2
    tm = _row_tile(s, tm)
    nb = f // tn
    return pl.pallas_call(
        _ffn_in_kernel,
        grid=(s // tm, nb),
        in_specs=[pl.BlockSpec((tm, k), lambda i, j: (i, 0)),
                  _w_spec(w_in, layer, k, tn),
                  _w_spec(w_in, layer, k, tn, nb)],
        out_specs=pl.BlockSpec((tm, tn), lambda i, j: (i, j)),
        out_shape=jax.ShapeDtypeStruct((s, f), BF16),
        compiler_params=_params(("parallel", "arbitrary")),
        name="ffn_in",
    )(xn, w_in, w_in)


def _k_chunk(k):
    for c in range(min(k, 5632), 0, -LANES):
        if c % LANES == 0 and k % c == 0:
            return c
    return k


def _mm_res_kernel(a_ref, w_ref, r_ref, o_ref, *, scale, k_chunk):
    acc = _dot_cast(a_ref, w_ref, k_chunk)
    o_ref[...] = r_ref[...] + scale * acc


def mm_res(a, w, layer, res, scale, tm=1024, tn=256):
    s, k = a.shape
    n = w.shape[-1]
    tm = _row_tile(s, tm)
    return pl.pallas_call(
        functools.partial(_mm_res_kernel, scale=scale, k_chunk=_k_chunk(k)),
        grid=(s // tm, n // tn),
        in_specs=[pl.BlockSpec((tm, k), lambda i, j: (i, 0), pipeline_mode=pl.Buffered(1)),
                  _w_spec(w, layer, k, tn),
                  pl.BlockSpec((tm, tn), lambda i, j: (i, j))],
        out_specs=pl.BlockSpec((tm, tn), lambda i, j: (i, j)),
        out_shape=jax.ShapeDtypeStruct((s, n), F32),
        compiler_params=_params(("parallel", "arbitrary")),
        name="mm_res",
    )(a, w, res)


def _mm_plain_kernel(a_ref, w_ref, o_ref, *, act):
    acc = _dot(a_ref[...], w_ref[...].astype(BF16))
    if act == "sigmoid":
        acc = _sigmoid(acc)
    o_ref[...] = acc.astype(o_ref.dtype)


def mm_plain(a, w, layer, n, tn, tm=1024, act=None, out_dtype=F32):
    s, k = a.shape
    tm = _row_tile(s, tm)
    return pl.pallas_call(
        functools.partial(_mm_plain_kernel, act=act),
        grid=(s // tm, n // tn),
        in_specs=[pl.BlockSpec((tm, k), lambda i, j: (i, 0)),
                  _w_spec(w, layer, k, tn)],
        out_specs=pl.BlockSpec((tm, tn), lambda i, j: (i, j)),
        out_shape=jax.ShapeDtypeStruct((s, n), out_dtype),
        compiler_params=_params(("parallel", "arbitrary")),
        name="mm_plain",
    )(a, w)


def rope_tables(pos):
    half = ROPE_DIM // 2
    inv = ROPE_THETA ** (-jnp.arange(half, dtype=F32) / half)
    ang = pos.astype(F32)[:, None] * inv[None, :]
    cos, sin = jnp.cos(ang), jnp.sin(ang)
    n = pos.shape[0]
    ones = jnp.ones((n, HEAD_DIM - ROPE_DIM), F32)
    zeros = jnp.zeros((n, HEAD_DIM - ROPE_DIM), F32)
    z16 = jnp.zeros((n, half), F32)
    cos_t = jnp.concatenate([cos, cos, ones], axis=1)
    sin_a = jnp.concatenate([z16, sin, zeros], axis=1)
    sin_b = jnp.concatenate([-sin, z16, zeros], axis=1)
    return cos_t, sin_a, sin_b


def _norm_rope(x, g, cos_t, sin_a, sin_b):
    ms = jnp.mean(x * x, axis=-1, keepdims=True)
    y = x * lax.rsqrt(ms + EPS) * g
    half = ROPE_DIM // 2
    return (y * cos_t + pltpu.roll(y, half, 1) * sin_a
            + pltpu.roll(y, HEAD_DIM - half, 1) * sin_b)


def _head_norm_rope_kernel(x_ref, g_ref, c_ref, sa_ref, sb_ref, o_ref, *, nh, scale):
    g = g_ref[...]
    c, sa, sb = c_ref[...], sa_ref[...], sb_ref[...]
    for h in range(nh):
        x = x_ref[:, h * HEAD_DIM:(h + 1) * HEAD_DIM]
        y = _norm_rope(x, g, c, sa, sb)
        o_ref[:, h * HEAD_DIM:(h + 1) * HEAD_DIM] = (y * scale).astype(o_ref.dtype)


def head_norm_rope(x, col0, nh, g, tables, scale=1.0, tm=256):
    s = x.shape[0]
    w = nh * HEAD_DIM
    cb = col0 // w
    tm = _row_tile(s, tm)
    tab_spec = pl.BlockSpec((tm, HEAD_DIM), lambda i: (i, 0))
    return pl.pallas_call(
        functools.partial(_head_norm_rope_kernel, nh=nh, scale=scale),
        grid=(s // tm,),
        in_specs=[pl.BlockSpec((tm, w), lambda i: (i, cb)),
                  pl.BlockSpec((1, HEAD_DIM), lambda i: (0, 0)),
                  tab_spec, tab_spec, tab_spec],
        out_specs=pl.BlockSpec((tm, w), lambda i: (i, 0)),
        out_shape=jax.ShapeDtypeStruct((s, w), BF16),
        compiler_params=_params(("parallel",)),
        name="head_norm_rope",
    )(x, g.reshape(1, HEAD_DIM), *tables)


def _compress_kernel(t_ref, pe_ref, w1_ref, w2_ref, g_ref, c_ref, sa_ref, sb_ref, o_ref, *, is_key):
    half = CMP_STRIDE * HEAD_DIM
    t = t_ref[0]
    top = (t + pe_ref[:, :half]).astype(BF16)
    bot = (t + pe_ref[:, half:]).astype(BF16)
    p = _dot(top, w1_ref[:half, :].astype(BF16))
    q = _dot(bot, w1_ref[half:, :].astype(BF16))
    n = t.shape[0]
    hid = p + pltpu.roll(q, n - 1, 0)
    out = _dot(_silu(hid).astype(BF16), w2_ref[...].astype(BF16))
    if is_key:
        out = _norm_rope(out, g_ref[...], c_ref[...], sa_ref[...], sb_ref[...])
    o_ref[0] = out.astype(o_ref.dtype)


def compress(t16, pe, w1, w2, g, tables, is_key):
    ng, n, wid = t16.shape
    tab_spec = pl.BlockSpec((n, HEAD_DIM), lambda i: (0, 0))
    return pl.pallas_call(
        functools.partial(_compress_kernel, is_key=is_key),
        grid=(ng,),
        in_specs=[pl.BlockSpec((1, n, wid), lambda i: (i, 0, 0)),
                  pl.BlockSpec((1, 2 * wid), lambda i: (0, 0)),
                  pl.BlockSpec((2 * wid, HEAD_DIM), lambda i: (0, 0)),
                  pl.BlockSpec((HEAD_DIM, HEAD_DIM), lambda i: (0, 0)),
                  pl.BlockSpec((1, HEAD_DIM), lambda i: (0, 0)),
                  tab_spec, tab_spec, tab_spec],
        out_specs=pl.BlockSpec((1, n, HEAD_DIM), lambda i: (i, 0, 0)),
        out_shape=jax.ShapeDtypeStruct((ng, n, HEAD_DIM), BF16),
        compiler_params=_params(("parallel",)),
        name="compress_k" if is_key else "compress_v",
    )(t16, pe.reshape(1, 2 * wid), w1, w2, g.reshape(1, HEAD_DIM), *tables)


def _conv_silu(x, carry_ref, w_ref, b_ref, first):
    @pl.when(first)
    def _():
        carry_ref[...] = jnp.zeros_like(carry_ref)

    full = jnp.concatenate([carry_ref[...], x], axis=0)
    n = x.shape[0]
    acc = b_ref[...] + w_ref[CONV_WIDTH - 1:CONV_WIDTH, :] * x
    for k in range(1, CONV_WIDTH):
        acc = acc + w_ref[CONV_WIDTH - 1 - k:CONV_WIDTH - k, :] * full[8 - k:8 - k + n]
    carry_ref[...] = x[n - 8:, :]
    return _silu(acc)


def _ssd_kernel(z_ref, x_ref, b_ref, c_ref, dt_ref, dtt_ref,
                cwx_ref, cwb_ref, cwc_ref, cbx_ref, cbb_ref, cbc_ref,
                bias_ref, biast_ref, alog_ref, alogt_ref, d_ref, ng_ref, spread_ref,
                o_ref,
                state_ref, xs_ref, y_ref, dtx_ref, ecx_ref, tex_ref, cx_ref, cb_ref, cc_ref):
    ci = pl.program_id(1)
    first = ci == 0
    L = CHUNK

    @pl.when(first)
    def _():
        state_ref[...] = jnp.zeros_like(state_ref)

    xg = _conv_silu(x_ref[...], cx_ref, cwx_ref, cbx_ref, first)
    bm = _conv_silu(b_ref[...], cb_ref, cwb_ref, cbb_ref, first)
    cm = _conv_silu(c_ref[...], cc_ref, cwc_ref, cbc_ref, first)

    dt = _softplus(dt_ref[0] + bias_ref[0])
    dtt = _softplus(dtt_ref[0] + biast_ref[0])
    a = dt * (-jnp.exp(alog_ref[0]))
    at = dtt * (-jnp.exp(alogt_ref[0]))

    row = lax.broadcasted_iota(jnp.int32, (L, L), 0)
    col = lax.broadcasted_iota(jnp.int32, (L, L), 1)
    causal = row >= col
    tril = causal.astype(BF16)
    triu = (row <= col).astype(BF16)
    cum = sum(_dot(tril, p) for p in _split3(a))
    cumt = sum(_dot(p, triu) for p in _split3(at))
    ecum = jnp.exp(cum)
    last = cum[L - 1:L, :]
    toend = jnp.exp(last - cum)
    elast = jnp.exp(last)

    bmb = bm.astype(BF16)
    cmb = cm.astype(BF16)
    cbmat = _dot_nt(cmb, bmb)
    yoff = _dot(cmb, state_ref[...].astype(BF16))

    spread = spread_ref[...]

    def per_channel(v):
        return sum(_dot(piece, spread) for piece in _split3(v))

    dtx_ref[...] = per_channel(dt)
    ecx_ref[...] = per_channel(ecum)
    tex_ref[...] = per_channel(toend)

    lane = lax.broadcasted_iota(jnp.int32, (L, LANES), 1)
    lo = lane < SSM_HEAD_DIM
    lane1 = lax.broadcasted_iota(jnp.int32, (1, LANES), 1)
    lo1 = lane1 < SSM_HEAD_DIM
    for k in range(SSM_HPG // 2):
        h0, h1 = 2 * k, 2 * k + 1
        sl = slice(k * LANES, (k + 1) * LANES)
        xp = xg[:, sl]
        xdt = xp * dtx_ref[:, sl]
        g0 = (cbmat * jnp.exp(jnp.where(causal, cum[:, h0:h0 + 1] - cumt[h0:h0 + 1, :], -jnp.inf))).astype(BF16)
        g1 = (cbmat * jnp.exp(jnp.where(causal, cum[:, h1:h1 + 1] - cumt[h1:h1 + 1, :], -jnp.inf))).astype(BF16)
        ydiag = (_dot(g0, jnp.where(lo, xdt, 0.0).astype(BF16))
                 + _dot(g1, jnp.where(lo, 0.0, xdt).astype(BF16)))
        y = ydiag + yoff[:, sl] * ecx_ref[:, sl] + d_ref[:, sl] * xp
        y_ref[:, sl] = y * _silu(z_ref[:, sl])
        xs_ref[:, sl] = (xdt * tex_ref[:, sl]).astype(BF16)
        state_ref[:, sl] = state_ref[:, sl] * jnp.where(lo1, elast[:, h0:h0 + 1], elast[:, h1:h1 + 1])

    state_ref[...] += _dot_tn(bmb, xs_ref[...])

    y = y_ref[...]
    ms = jnp.mean(y * y, axis=-1, keepdims=True)
    o_ref[...] = (y * lax.rsqrt(ms + EPS) * ng_ref[...]).astype(o_ref.dtype)


def ssd_mixer_core(zx, bcdt, conv_w, conv_b, dt_bias, a_log, d_skip, norm_g):
    s = zx.shape[0]
    nc = s // CHUNK
    gw, n, hpg, ng = SSM_GROUP_W, SSM_STATE, SSM_HPG, SSM_GROUPS
    dt_raw = bcdt[:, 2 * ng * n:2 * ng * n + ng * hpg]
    pad = ((0, 0), (0, 0), (0, LANES - hpg))
    dt_g = jnp.pad(dt_raw.reshape(s, ng, hpg).transpose(1, 0, 2), pad)
    dt_gt = dt_raw.reshape(s, ng, hpg).transpose(1, 2, 0)

    def per_head(v):
        return jnp.pad(v.reshape(ng, 1, hpg), pad), v.reshape(ng, hpg, 1)

    bias, biast = per_head(dt_bias)
    alog, alogt = per_head(a_log)
    d_ch = jnp.repeat(d_skip, SSM_HEAD_DIM).reshape(1, D_INNER)
    spread = jnp.asarray(np.kron(np.eye(LANES, hpg), np.ones((1, SSM_HEAD_DIM))), BF16)
    cb2 = conv_b.reshape(1, -1)
    xoff, boff, coff = D_INNER // gw, D_INNER // n, D_INNER // n + ng
    head_spec = pl.BlockSpec((1, 1, LANES), lambda g, c: (g, 0, 0))
    headt_spec = pl.BlockSpec((1, hpg, 1), lambda g, c: (g, 0, 0))
    return pl.pallas_call(
        _ssd_kernel,
        grid=(ng, nc),
        in_specs=[
            pl.BlockSpec((CHUNK, gw), lambda g, c: (c, g)),
            pl.BlockSpec((CHUNK, gw), lambda g, c: (c, xoff + g)),
            pl.BlockSpec((CHUNK, n), lambda g, c: (c, g)),
            pl.BlockSpec((CHUNK, n), lambda g, c: (c, ng + g)),
            pl.BlockSpec((1, CHUNK, LANES), lambda g, c: (g, c, 0)),
            pl.BlockSpec((1, hpg, CHUNK), lambda g, c: (g, 0, c)),
            pl.BlockSpec((CONV_WIDTH, gw), lambda g, c: (0, g)),
            pl.BlockSpec((CONV_WIDTH, n), lambda g, c: (0, boff + g)),
            pl.BlockSpec((CONV_WIDTH, n), lambda g, c: (0, coff + g)),
            pl.BlockSpec((1, gw), lambda g, c: (0, g)),
            pl.BlockSpec((1, n), lambda g, c: (0, boff + g)),
            pl.BlockSpec((1, n), lambda g, c: (0, coff + g)),
            head_spec, headt_spec, head_spec, headt_spec,
            pl.BlockSpec((1, gw), lambda g, c: (0, g)),
            pl.BlockSpec((1, gw), lambda g, c: (0, g)),
            pl.BlockSpec((LANES, gw), lambda g, c: (0, 0)),
        ],
        out_specs=pl.BlockSpec((CHUNK, gw), lambda g, c: (c, g)),
        out_shape=jax.ShapeDtypeStruct((s, D_INNER), BF16),
        scratch_shapes=[
            pltpu.VMEM((n, gw), F32),
            pltpu.VMEM((CHUNK, gw), BF16),
            pltpu.VMEM((CHUNK, gw), F32),
            pltpu.VMEM((CHUNK, gw), F32),
            pltpu.VMEM((CHUNK, gw), F32),
            pltpu.VMEM((CHUNK, gw), F32),
            pltpu.VMEM((8, gw), F32),
            pltpu.VMEM((8, n), F32),
            pltpu.VMEM((8, n), F32),
        ],
        compiler_params=_params(("parallel", "arbitrary")),
        name="ssd",
    )(zx, zx, bcdt, bcdt, dt_g, dt_gt, conv_w, conv_w, conv_w, cb2, cb2, cb2,
      bias, biast, alog, alogt, d_ch, norm_g.reshape(1, -1), spread)


SEL_TILE = 256
SEL_SHIFT = SEL_BLOCK.bit_length() - 1
WIN_TILES = WINDOW // Q_BLOCK + 1
WIN_KEYS = WIN_TILES * Q_BLOCK
SUM_FLOOR = 1e-30


def _selection_matrix(n_cmp_pad, n_sblk):
    ratio = SEL_BLOCK // CMP_STRIDE
    offs = np.arange(-(CMP_BLOCK // CMP_STRIDE), ratio + 1)
    ov = np.clip(np.minimum(offs * CMP_STRIDE + CMP_BLOCK, SEL_BLOCK) - np.maximum(offs * CMP_STRIDE, 0), 0, None)
    m = np.zeros((n_cmp_pad, n_sblk), np.float32)
    for o, w in zip(offs, ov):
        if w > 0:
            for j in range(n_sblk):
                c = j * ratio + o
                if 0 <= c < n_cmp_pad - 1:
                    m[c, j] = w / CMP_BLOCK
    return m


def _rows(r):
    return slice(r * Q_BLOCK, (r + 1) * Q_BLOCK)


def _lane_fold(x, op):
    out = x[:, :LANES]
    for f in range(1, x.shape[1] // LANES):
        out = op(out, x[:, f * LANES:(f + 1) * LANES])
    return out


def _tile_lanes(x, width):
    return jnp.concatenate([x] * (width // LANES), axis=1)


def _row_stat(x, op, reduce):
    return jnp.broadcast_to(reduce(_lane_fold(x, op), axis=-1, keepdims=True), (x.shape[0], LANES))


def _nsa_kernel(q_ref, gate_ref, kc_ref, vc_ref, ks_ref, vs_ref, kw_ref, vw_ref, selmt_ref, shift_ref,
                o_ref,
                bias_ref, q_scr, s_ref, p_ref, s2_ref, p2_ref, b2_ref, psum_ref, stm_ref, stl_ref, wl_ref,
                mx_ref, lacc_ref, acc_ref, oc_ref, ow_ref):
    qb = pl.program_id(1)
    t0 = qb * Q_BLOCK
    R, QB, D = Q_PER_KV, Q_BLOCK, HEAD_DIM
    n_cmp = kc_ref.shape[1]
    n_sblk = selmt_ref.shape[0]

    tok = t0 + lax.broadcasted_iota(jnp.int32, (QB, 1), 0)
    for r in range(R):
        q_scr[_rows(r), :] = q_ref[:, r * D:(r + 1) * D]

    cend = lax.broadcasted_iota(jnp.int32, (QB, n_cmp), 1) * CMP_STRIDE + (CMP_BLOCK - 1)
    b2_ref[:, :n_cmp] = jnp.where(cend <= tok, 0.0, NEG)
    s_ref[:, :n_cmp] = _dot_nt(q_scr[...], kc_ref[0])
    for r in range(R):
        stm_ref[_rows(r)] = _row_stat(s_ref[_rows(r), :n_cmp] + b2_ref[:, :n_cmp], jnp.maximum, jnp.max)
    for r in range(R):
        e = jnp.exp(s_ref[_rows(r), :n_cmp] + b2_ref[:, :n_cmp] - _tile_lanes(stm_ref[_rows(r)], n_cmp))
        s_ref[_rows(r), :n_cmp] = e
        stl_ref[_rows(r)] = _row_stat(e, jnp.add, jnp.sum)
    has_key = tok >= CMP_BLOCK - 1
    psum_ref[...] = jnp.zeros(psum_ref.shape, F32)
    for r in range(R):
        inv = jnp.where(has_key, 1.0 / stl_ref[_rows(r)], 0.0)
        p = s_ref[_rows(r), :n_cmp] * _tile_lanes(inv, n_cmp)
        psum_ref[...] += p
        p_ref[_rows(r), :n_cmp] = p.astype(BF16)
    oc_ref[...] = _dot(p_ref[:, :n_cmp], vc_ref[0])
    psum = psum_ref[...]

    kws, vws = [], []
    for i in range(WIN_TILES):
        tile = qb - (WIN_TILES - 1) + i
        k0 = pl.multiple_of(jnp.maximum(tile, 0) * QB, QB)
        kws.append(kw_ref[pl.ds(k0, QB), :])
        vws.append(vw_ref[pl.ds(k0, QB), :])
        spos = tile * QB + lax.broadcasted_iota(jnp.int32, (QB, QB), 1)
        ok = (spos <= tok) & (spos > tok - WINDOW) & (spos >= 0)
        b2_ref[:, i * QB:(i + 1) * QB] = jnp.where(ok, 0.0, NEG)
    s_ref[:, :WIN_KEYS] = _dot_nt(q_scr[...], jnp.concatenate(kws, axis=0))
    for r in range(R):
        stm_ref[_rows(r)] = _row_stat(s_ref[_rows(r), :WIN_KEYS] + b2_ref[:, :WIN_KEYS], jnp.maximum, jnp.max)
    for r in range(R):
        e = jnp.exp(s_ref[_rows(r), :WIN_KEYS] + b2_ref[:, :WIN_KEYS] - _tile_lanes(stm_ref[_rows(r)], WIN_KEYS))
        p_ref[_rows(r), :WIN_KEYS] = e.astype(BF16)
        wl_ref[_rows(r)] = _row_stat(e, jnp.add, jnp.sum)
    ow_ref[...] = _dot(p_ref[:, :WIN_KEYS], jnp.concatenate(vws, axis=0))

    selmt = selmt_ref[...].astype(BF16)
    imp = sum(_dot_nt(selmt, piece) for piece in _split3(psum))
    jj = lax.broadcasted_iota(jnp.int32, (n_sblk, QB), 0)
    jf = jj.astype(F32)
    tok_t = t0 + lax.broadcasted_iota(jnp.int32, (1, QB), 1)
    cur = jnp.right_shift(tok_t, SEL_SHIFT)
    forced = (jj == 0) | (jj == cur) | (jj == cur - 1)
    score = jnp.where(jj <= cur, imp + jnp.where(forced, FORCE_BONUS, 0.0), NEG)
    sel_t = jnp.zeros((n_sblk, QB), F32)
    for _ in range(min(N_SELECT, n_sblk)):
        mx = jnp.max(score, axis=0, keepdims=True)
        idx = jnp.min(jnp.where(score == mx, jf, float(n_sblk)), axis=0, keepdims=True)
        hit = jf == idx
        sel_t = jnp.where(hit, 1.0, sel_t)
        score = jnp.where(hit, -jnp.inf, score)
    selb = sel_t.T.astype(BF16)

    T = SEL_TILE
    n_pairs = (t0 + QB + 2 * T - 1) // (2 * T)
    last_tile = ks_ref.shape[0] // T - 1
    shift = shift_ref[...]

    def tile_start(kt):
        return pl.multiple_of(jnp.minimum(kt, last_tile) * T, T)

    def scores(kt):
        return _dot_nt(q_scr[...], ks_ref[pl.ds(tile_start(kt), T), :])

    def build_bias(kt):
        k0 = kt * T
        kpos = k0 + lax.broadcasted_iota(jnp.int32, (n_sblk, T), 1)
        blk = lax.broadcasted_iota(jnp.int32, (n_sblk, T), 0)
        expand = jnp.where(jnp.right_shift(kpos, SEL_SHIFT) == blk, 1.0, 0.0).astype(BF16)
        picked = _dot(selb, expand)
        kq = k0 + lax.broadcasted_iota(jnp.int32, (QB, T), 1)
        bias_ref[kt] = jnp.where((picked > 0.5) & (kq <= tok), -shift, NEG)

    def exp_tile(buf, pbuf, kt, use_max):
        for r in range(R):
            x = buf[_rows(r), :T] + bias_ref[kt]
            if use_max:
                x = x - _tile_lanes(mx_ref[_rows(r)], T)
            e = jnp.exp(x)
            lacc_ref[_rows(r)] += _lane_fold(e, jnp.add)
            pbuf[_rows(r), :T] = e.astype(BF16)

    def sum_pass(use_max, build):
        lacc_ref[...] = jnp.zeros(lacc_ref.shape, F32)
        acc_ref[...] = jnp.zeros(acc_ref.shape, F32)

        def body(j, c):
            a = 2 * j
            s2_ref[...] = scores(a + 1)
            if build:
                build_bias(a + 2)
                build_bias(a + 3)
            exp_tile(s_ref, p_ref, a, use_max)
            acc_ref[...] += _dot(p_ref[:, :T], vs_ref[pl.ds(tile_start(a), T), :])
            s_ref[:, :T] = scores(a + 2)
            exp_tile(s2_ref, p2_ref, a + 1, use_max)
            acc_ref[...] += _dot(p2_ref[...], vs_ref[pl.ds(tile_start(a + 1), T), :])
            return c

        s_ref[:, :T] = scores(0)
        if build:
            build_bias(0)
            build_bias(1)
        lax.fori_loop(0, n_pairs, body, 0)

    sum_pass(use_max=False, build=True)
    l_min = jnp.min(jnp.sum(lacc_ref[...], axis=-1, keepdims=True))

    @pl.when(l_min < SUM_FLOOR)
    def _():
        mx_ref[...] = jnp.full(mx_ref.shape, NEG, F32)

        def fold_max(buf, kt):
            for r in range(R):
                m = _lane_fold(buf[_rows(r), :T] + bias_ref[kt], jnp.maximum)
                mx_ref[_rows(r)] = jnp.maximum(mx_ref[_rows(r)], m)

        def max_body(j, c):
            a = 2 * j
            s2_ref[...] = scores(a + 1)
            fold_max(s_ref, a)
            s_ref[:, :T] = scores(a + 2)
            fold_max(s2_ref, a + 1)
            return c

        s_ref[:, :T] = scores(0)
        lax.fori_loop(0, n_pairs, max_body, 0)
        for r in range(R):
            mx_ref[_rows(r)] = jnp.broadcast_to(jnp.max(mx_ref[_rows(r)], axis=-1, keepdims=True), (QB, LANES))
        sum_pass(use_max=True, build=False)

    gates = gate_ref[...]
    for r in range(R):
        osel = acc_ref[_rows(r)] * (1.0 / jnp.sum(lacc_ref[_rows(r)], axis=-1, keepdims=True))
        ow = ow_ref[_rows(r)] * (1.0 / wl_ref[_rows(r)])
        o = (gates[:, 3 * r:3 * r + 1] * oc_ref[_rows(r)] + gates[:, 3 * r + 1:3 * r + 2] * osel
             + gates[:, 3 * r + 2:3 * r + 3] * ow)
        o_ref[:, r * D:(r + 1) * D] = o.astype(o_ref.dtype)


def nsa_attention(q, gates, kc, vc, ks, vs, kw, vw, score_bound):
    s = q.shape[0]
    shift = jnp.full((1, SEL_TILE), score_bound, F32)
    nqb = s // Q_BLOCK
    n_cmp = kc.shape[1]
    n_sblk = s // SEL_BLOCK
    gw = Q_PER_KV * HEAD_DIM
    selmt = jnp.asarray(_selection_matrix(n_cmp, n_sblk).T)
    kv_spec = pl.BlockSpec((s, HEAD_DIM), lambda g, i: (0, g))
    cmp_spec = pl.BlockSpec((1, n_cmp, HEAD_DIM), lambda g, i: (g, 0, 0))
    rows = Q_PER_KV * Q_BLOCK
    wide = max(WIN_KEYS, n_cmp, SEL_TILE)
    return pl.pallas_call(
        _nsa_kernel,
        grid=(KV_GROUPS, nqb),
        in_specs=[pl.BlockSpec((Q_BLOCK, gw), lambda g, i: (i, g)),
                  pl.BlockSpec((Q_BLOCK, LANES), lambda g, i: (i, g)),
                  cmp_spec, cmp_spec, kv_spec, kv_spec, kv_spec, kv_spec,
                  pl.BlockSpec((n_sblk, n_cmp), lambda g, i: (0, 0)),
                  pl.BlockSpec((1, SEL_TILE), lambda g, i: (0, 0))],
        out_specs=pl.BlockSpec((Q_BLOCK, gw), lambda g, i: (i, g)),
        out_shape=jax.ShapeDtypeStruct((s, ATTN_HEADS * HEAD_DIM), BF16),
        scratch_shapes=[
            pltpu.VMEM((s // SEL_TILE + 2, Q_BLOCK, SEL_TILE), F32),
            pltpu.VMEM((rows, HEAD_DIM), BF16),
            pltpu.VMEM((rows, wide), F32),
            pltpu.VMEM((rows, wide), BF16),
            pltpu.VMEM((rows, SEL_TILE), F32),
            pltpu.VMEM((rows, SEL_TILE), BF16),
            pltpu.VMEM((Q_BLOCK, wide), F32),
            pltpu.VMEM((Q_BLOCK, n_cmp), F32),
            pltpu.VMEM((rows, LANES), F32),
            pltpu.VMEM((rows, LANES), F32),
            pltpu.VMEM((rows, LANES), F32),
            pltpu.VMEM((rows, LANES), F32),
            pltpu.VMEM((rows, LANES), F32),
            pltpu.VMEM((rows, HEAD_DIM), F32),
            pltpu.VMEM((rows, HEAD_DIM), F32),
            pltpu.VMEM((rows, HEAD_DIM), F32),
        ],
        compiler_params=_params(("parallel", "arbitrary")),
        name="nsa_attention",
    )(q, gates, kc, vc, ks, vs, kw, vw, selmt, shift)


def _ffn(h, norm_g, w_in, w_out, layer):
    act = ffn_in(rmsnorm(h, norm_g[layer]), w_in, layer)
    return mm_res(act, w_out, layer, h, 0.5)


def _mamba(h, norm_g, w_in, conv_w, conv_b, dt_bias, a_log, d_skip, ssm_norm, w_out, layer):
    u = rmsnorm(h, norm_g)
    zx_cols = 2 * D_INNER
    zx = mm_plain(u, w_in, layer, zx_cols, 512)
    w_tail = w_in[layer, :, zx_cols:]
    w_tail = jnp.pad(w_tail, ((0, 0), (0, -w_tail.shape[1] % 256)))
    bcdt = mm_plain(u, w_tail, 0, w_tail.shape[1], 256)
    y = ssd_mixer_core(zx, bcdt, conv_w[layer], conv_b[layer], dt_bias[layer], a_log[layer], d_skip[layer],
                       ssm_norm[layer])
    return mm_res(y, w_out, layer, h, 1.0)


def _shared_kv(h, kv_norm, kv_w, cmp_pe_k, cmp_w1_k, cmp_w2_k, cmp_pe_v, cmp_w1_v, cmp_w2_v,
               k_norm_cmp, k_norm_slc, k_norm_win):
    s = h.shape[0]
    gd = KV_GROUPS * HEAD_DIM
    kv = mm_plain(rmsnorm(h, kv_norm), kv_w, 0, 6 * gd, 512)
    n16 = s // CMP_STRIDE

    def blocks16(t):
        return t.reshape(n16, CMP_STRIDE, KV_GROUPS, HEAD_DIM).transpose(2, 0, 1, 3).reshape(
            KV_GROUPS, n16, CMP_STRIDE * HEAD_DIM)

    cmp_tables = rope_tables(jnp.arange(n16) * CMP_STRIDE + (CMP_BLOCK - 1))
    pos_tables = rope_tables(jnp.arange(s))
    kc = compress(blocks16(kv[:, 0:gd]), cmp_pe_k, cmp_w1_k, cmp_w2_k, k_norm_cmp, cmp_tables, True)
    vc = compress(blocks16(kv[:, gd:2 * gd]), cmp_pe_v, cmp_w1_v, cmp_w2_v, k_norm_cmp, cmp_tables, False)
    ks = head_norm_rope(kv, 2 * gd, KV_GROUPS, k_norm_slc, pos_tables)
    kw = head_norm_rope(kv, 4 * gd, KV_GROUPS, k_norm_win, pos_tables)
    vs = kv[:, 3 * gd:4 * gd].astype(BF16)
    vw = kv[:, 5 * gd:6 * gd].astype(BF16)
    k_bound = math.sqrt(HEAD_DIM) * jnp.max(jnp.abs(k_norm_slc))
    return kc, vc, ks, vs, kw, vw, pos_tables, k_bound


def _nsa(h, norm_g, w_qg, q_norm_g, w_o, shared, layer):
    kc, vc, ks, vs, kw, vw, pos_tables, k_bound = shared
    u = rmsnorm(h, norm_g)
    qd = ATTN_HEADS * HEAD_DIM
    q = mm_plain(u, w_qg, layer, qd, 512)
    per_g = Q_PER_KV * 3
    w_g = w_qg[layer, :, qd:].reshape(D_MODEL, KV_GROUPS, per_g)
    w_g = jnp.pad(w_g, ((0, 0), (0, 0), (0, LANES - per_g))).reshape(D_MODEL, KV_GROUPS * LANES)
    gates = mm_plain(u, w_g, 0, KV_GROUPS * LANES, KV_GROUPS * LANES, act="sigmoid")
    scale = HEAD_DIM ** -0.5
    qn = head_norm_rope(q, 0, ATTN_HEADS, q_norm_g[layer], pos_tables, scale=scale)
    q_bound = math.sqrt(HEAD_DIM) * scale * jnp.max(jnp.abs(q_norm_g[layer]))
    o = nsa_attention(qn, gates, kc, vc, ks, vs, kw, vw, 1.01 * q_bound * k_bound)
    return mm_res(o, w_o, layer, h, 1.0, tn=512)


def kernel(x, ffn_a_norm, ffn_a_w_in, ffn_a_w_out, ffn_b_norm, ffn_b_w_in, ffn_b_w_out, mix_norm,
           ssm_w_in, ssm_conv_w, ssm_conv_b, ssm_dt_bias, ssm_a_log, ssm_d, ssm_norm, ssm_w_out,
           kv_norm, kv_w, cmp_pe_k, cmp_w1_k, cmp_w2_k, cmp_pe_v, cmp_w1_v, cmp_w2_v,
           k_norm_cmp, k_norm_slc, k_norm_win, attn_w_qg, attn_q_norm, attn_w_o):
    b, s, d = x.shape
    n_layers = ffn_a_norm.shape[0]
    n_a = ssm_w_in.shape[0]
    outs = []
    for bi in range(b):
        h = x.reshape(s, d) if b == 1 else x[bi]
        shared = None
        for i in range(n_layers):
            h = _ffn(h, ffn_a_norm, ffn_a_w_in, ffn_a_w_out, i)
            if i < n_a:
                h = _mamba(h, mix_norm[i], ssm_w_in, ssm_conv_w, ssm_conv_b, ssm_dt_bias,
                           ssm_a_log, ssm_d, ssm_norm, ssm_w_out, i)
            else:
                h = _nsa(h, mix_norm[i], attn_w_qg, attn_q_norm, attn_w_o, shared, i - n_a)
            h = _ffn(h, ffn_b_norm, ffn_b_w_in, ffn_b_w_out, i)
            if i == n_a - 1:
                shared = _shared_kv(h, kv_norm, kv_w, cmp_pe_k, cmp_w1_k, cmp_w2_k, cmp_pe_v, cmp_w1_v,
                                    cmp_w2_v, k_norm_cmp, k_norm_slc, k_norm_win)
        outs.append(h)
    return jnp.stack(outs, axis=0)
```

```python
import functools
import math

import jax
import jax.numpy as jnp
import numpy as np
from jax import lax
from jax.experimental import pallas as pl
from jax.experimental.pallas import tpu as pltpu

F32 = jnp.float32
BF16 = jnp.bfloat16

D_MODEL = 4096
D_FF = 11008
D_INNER = 8192
SSM_HEAD_DIM = 64
SSM_GROUPS = 8
SSM_HPG = 16
SSM_STATE = 128
SSM_GROUP_W = SSM_HPG * SSM_HEAD_DIM
CONV_WIDTH = 4
CHUNK = 128
ATTN_HEADS = 32
KV_GROUPS = 4
Q_PER_KV = 8
HEAD_DIM = 128
CMP_BLOCK = 32
CMP_STRIDE = 16
SEL_BLOCK = 64
N_SELECT = 16
WINDOW = 512
Q_BLOCK = 128
ROPE_THETA = 500000.0
ROPE_DIM = 32
EPS = 1e-6
NEG = -1e30
FORCE_BONUS = 1e4

LANES = 128
VMEM_LIMIT = 56 * 1024 * 1024


def _params(sem):
    return pltpu.CompilerParams(dimension_semantics=sem, vmem_limit_bytes=VMEM_LIMIT)


def _dot(a, b):
    return jnp.dot(a, b, preferred_element_type=F32)


def _dot_nt(a, b):
    return lax.dot_general(a, b, (((1,), (1,)), ((), ())), preferred_element_type=F32)


def _dot_tn(a, b):
    return lax.dot_general(a, b, (((0,), (0,)), ((), ())), preferred_element_type=F32)


def _split3(x):
    hi = x.astype(BF16)
    r1 = x - hi.astype(F32)
    mid = r1.astype(BF16)
    lo = (r1 - mid.astype(F32)).astype(BF16)
    return hi, mid, lo


def _sigmoid(x):
    return 0.5 * jnp.tanh(0.5 * x) + 0.5


def _silu(x):
    return x * _sigmoid(x)


def _softplus(x):
    return jnp.maximum(x, 0.0) + jnp.log1p(jnp.exp(-jnp.abs(x)))


def _dot_cast(a_ref, w_ref, k_chunk):
    k = a_ref.shape[1]
    acc = None
    for k0 in range(0, k, k_chunk):
        part = _dot(a_ref[:, k0:k0 + k_chunk], w_ref[k0:k0 + k_chunk, :].astype(BF16))
        acc = part if acc is None else acc + part
    return acc


def _rmsnorm_kernel(x_ref, g_ref, o_ref):
    x = x_ref[...]
    ms = jnp.mean(x * x, axis=-1, keepdims=True)
    o_ref[...] = (x * lax.rsqrt(ms + EPS) * g_ref[...]).astype(o_ref.dtype)


def rmsnorm(x, g, tm=256):
    s, d = x.shape
    return pl.pallas_call(
        _rmsnorm_kernel,
        grid=(s // tm,),
        in_specs=[pl.BlockSpec((tm, d), lambda i: (i, 0)),
                  pl.BlockSpec((1, d), lambda i: (0, 0))],
        out_specs=pl.BlockSpec((tm, d), lambda i: (i, 0)),
        out_shape=jax.ShapeDtypeStruct((s, d), BF16),
        compiler_params=_params(("parallel",)),
        name="rmsnorm",
    )(x, g.reshape(1, d))


def _row_tile(s, want):
    return min(s, want)


def _w_spec(w, layer, k, tn, col0=0):
    if w.ndim == 2:
        return pl.BlockSpec((k, tn), lambda i, j: (0, j + col0))
    return pl.BlockSpec((None, k, tn), lambda i, j: (layer, 0, j + col0))


def _ffn_in_kernel(x_ref, wa_ref, wb_ref, wo_ref, o_ref, wob_ref):
    x = x_ref[...]
    a = _dot(x, wa_ref[...].astype(BF16))
    b = _dot(x, wb_ref[...].astype(BF16))
    o_ref[...] = (_silu(a) * b).astype(o_ref.dtype)
    wob_ref[...] = wo_ref[...].astype(BF16)


def ffn_in(xn, w_in, w_out, layer, tm=1024, tn=256):
    s, k = xn.shape
    f = w_in.shape[-1] // 2
    tm = _row_tile(s, tm)
    nb = f // tn
    n_steps = (s // tm) * nb
    fo, d = w_out.shape[-2:]
    assert fo % n_steps == 0 and (fo // n_steps) % 16 == 0, (fo, n_steps)
    rows = fo // n_steps
    return pl.pallas_call(
        _ffn_in_kernel,
        grid=(s // tm, nb),
        in_specs=[pl.BlockSpec((tm, k), lambda i, j: (i, 0)),
                  _w_spec(w_in, layer, k, tn),
                  _w_spec(w_in, layer, k, tn, nb),
                  pl.BlockSpec((None, rows, d), lambda i, j: (layer, i * nb + j, 0))],
        out_specs=[pl.BlockSpec((tm, tn), lambda i, j: (i, j)),
                   pl.BlockSpec((rows, d), lambda i, j: (i * nb + j, 0))],
        out_shape=[jax.ShapeDtypeStruct((s, f), BF16), jax.ShapeDtypeStruct((fo, d), BF16)],
        compiler_params=_params(("parallel", "arbitrary")),
        name="ffn_in",
    )(xn, w_in, w_in, w_out)


def _mm_res_bf16_kernel(a_ref, w_ref, r_ref, o_ref, *, scale):
    o_ref[...] = r_ref[...] + scale * _dot(a_ref[...], w_ref[...])


def mm_res_bf16(a, w, res, scale, tm=512, tn=512):
    s, k = a.shape
    n = w.shape[1]
    tm = _row_tile(s, tm)
    return pl.pallas_call(
        functools.partial(_mm_res_bf16_kernel, scale=scale),
        grid=(s // tm, n // tn),
        in_specs=[pl.BlockSpec((tm, k), lambda i, j: (i, 0)),
                  pl.BlockSpec((k, tn), lambda i, j: (0, j)),
                  pl.BlockSpec((tm, tn), lambda i, j: (i, j))],
        out_specs=pl.BlockSpec((tm, tn), lambda i, j: (i, j)),
        out_shape=jax.ShapeDtypeStruct((s, n), F32),
        compiler_params=_params(("parallel", "arbitrary")),
        name="mm_res_bf16",
    )(a, w, res)


def _k_chunk(k):
    for c in range(min(k, 5632), 0, -LANES):
        if c % LANES == 0 and k % c == 0:
            return c
    return k


def _mm_res_kernel(a_ref, w_ref, r_ref, o_ref, *, scale, k_chunk):
    acc = _dot_cast(a_ref, w_ref, k_chunk)
    o_ref[...] = r_ref[...] + scale * acc


def mm_res(a, w, layer, res, scale, tm=1024, tn=256):
    s, k = a.shape
    n = w.shape[-1]
    tm = _row_tile(s, tm)
    return pl.pallas_call(
        functools.partial(_mm_res_kernel, scale=scale, k_chunk=_k_chunk(k)),
        grid=(s // tm, n // tn),
        in_specs=[pl.BlockSpec((tm, k), lambda i, j: (i, 0), pipeline_mode=pl.Buffered(1)),
                  _w_spec(w, layer, k, tn),
                  pl.BlockSpec((tm, tn), lambda i, j: (i, j))],
        out_specs=pl.BlockSpec((tm, tn), lambda i, j: (i, j)),
        out_shape=jax.ShapeDtypeStruct((s, n), F32),
        compiler_params=_params(("parallel", "arbitrary")),
        name="mm_res",
    )(a, w, res)


def _mm_plain_kernel(a_ref, w_ref, o_ref, *, act):
    acc = _dot(a_ref[...], w_ref[...].astype(BF16))
    if act == "sigmoid":
        acc = _sigmoid(acc)
    o_ref[...] = acc.astype(o_ref.dtype)


def mm_plain(a, w, layer, n, tn, tm=1024, act=None, out_dtype=F32):
    s, k = a.shape
    tm = _row_tile(s, tm)
    return pl.pallas_call(
        functools.partial(_mm_plain_kernel, act=act),
        grid=(s // tm, n // tn),
        in_specs=[pl.BlockSpec((tm, k), lambda i, j: (i, 0)),
                  _w_spec(w, layer, k, tn)],
        out_specs=pl.BlockSpec((tm, tn), lambda i, j: (i, j)),
        out_shape=jax.ShapeDtypeStruct((s, n), out_dtype),
        compiler_params=_params(("parallel", "arbitrary")),
        name="mm_plain",
    )(a, w)


def rope_tables(pos):
    half = ROPE_DIM // 2
    inv = ROPE_THETA ** (-jnp.arange(half, dtype=F32) / half)
    ang = pos.astype(F32)[:, None] * inv[None, :]
    cos, sin = jnp.cos(ang), jnp.sin(ang)
    n = pos.shape[0]
    ones = jnp.ones((n, HEAD_DIM - ROPE_DIM), F32)
    zeros = jnp.zeros((n, HEAD_DIM - ROPE_DIM), F32)
    z16 = jnp.zeros((n, half), F32)
    cos_t = jnp.concatenate([cos, cos, ones], axis=1)
    sin_a = jnp.concatenate([z16, sin, zeros], axis=1)
    sin_b = jnp.concatenate([-sin, z16, zeros], axis=1)
    return cos_t, sin_a, sin_b


def _norm_rope(x, g, cos_t, sin_a, sin_b):
    ms = jnp.mean(x * x, axis=-1, keepdims=True)
    y = x * lax.rsqrt(ms + EPS) * g
    half = ROPE_DIM // 2
    return (y * cos_t + pltpu.roll(y, half, 1) * sin_a
            + pltpu.roll(y, HEAD_DIM - half, 1) * sin_b)


def _head_norm_rope_kernel(x_ref, g_ref, c_ref, sa_ref, sb_ref, o_ref, *, nh, scale):
    g = g_ref[...]
    c, sa, sb = c_ref[...], sa_ref[...], sb_ref[...]
    for h in range(nh):
        x = x_ref[:, h * HEAD_DIM:(h + 1) * HEAD_DIM]
        y = _norm_rope(x, g, c, sa, sb)
        o_ref[:, h * HEAD_DIM:(h + 1) * HEAD_DIM] = (y * scale).astype(o_ref.dtype)


def _proj_norm_rope_kernel(a_ref, w_ref, g_ref, c_ref, sa_ref, sb_ref, o_ref, *, nh, scale):
    acc = _dot(a_ref[...], w_ref[...].astype(BF16))
    g = g_ref[...]
    c, sa, sb = c_ref[...], sa_ref[...], sb_ref[...]
    for h in range(nh):
        y = _norm_rope(acc[:, h * HEAD_DIM:(h + 1) * HEAD_DIM], g, c, sa, sb)
        o_ref[:, h * HEAD_DIM:(h + 1) * HEAD_DIM] = (y * scale).astype(o_ref.dtype)


def proj_norm_rope(a, w, layer, n, g, tables, scale, tm=1024, nh=4):
    s, k = a.shape
    tm = _row_tile(s, tm)
    tn = nh * HEAD_DIM
    tab_spec = pl.BlockSpec((tm, HEAD_DIM), lambda i, j: (i, 0))
    return pl.pallas_call(
        functools.partial(_proj_norm_rope_kernel, nh=nh, scale=scale),
        grid=(s // tm, n // tn),
        in_specs=[pl.BlockSpec((tm, k), lambda i, j: (i, 0)),
                  _w_spec(w, layer, k, tn),
                  pl.BlockSpec((1, HEAD_DIM), lambda i, j: (0, 0)),
                  tab_spec, tab_spec, tab_spec],
        out_specs=pl.BlockSpec((tm, tn), lambda i, j: (i, j)),
        out_shape=jax.ShapeDtypeStruct((s, n), BF16),
        compiler_params=_params(("parallel", "arbitrary")),
        name="proj_norm_rope",
    )(a, w, g.reshape(1, HEAD_DIM), *tables)


def head_norm_rope(x, col0, nh, g, tables, scale=1.0, tm=256):
    s = x.shape[0]
    w = nh * HEAD_DIM
    cb = col0 // w
    tm = _row_tile(s, tm)
    tab_spec = pl.BlockSpec((tm, HEAD_DIM), lambda i: (i, 0))
    return pl.pallas_call(
        functools.partial(_head_norm_rope_kernel, nh=nh, scale=scale),
        grid=(s // tm,),
        in_specs=[pl.BlockSpec((tm, w), lambda i: (i, cb)),
                  pl.BlockSpec((1, HEAD_DIM), lambda i: (0, 0)),
                  tab_spec, tab_spec, tab_spec],
        out_specs=pl.BlockSpec((tm, w), lambda i: (i, 0)),
        out_shape=jax.ShapeDtypeStruct((s, w), BF16),
        compiler_params=_params(("parallel",)),
        name="head_norm_rope",
    )(x, g.reshape(1, HEAD_DIM), *tables)


def _compress_kernel(t_ref, pe_ref, w1_ref, w2_ref, g_ref, c_ref, sa_ref, sb_ref, o_ref, *, is_key):
    half = CMP_STRIDE * HEAD_DIM
    t = t_ref[0]
    top = (t + pe_ref[:, :half]).astype(BF16)
    bot = (t + pe_ref[:, half:]).astype(BF16)
    p = _dot(top, w1_ref[:half, :].astype(BF16))
    q = _dot(bot, w1_ref[half:, :].astype(BF16))
    n = t.shape[0]
    hid = p + pltpu.roll(q, n - 1, 0)
    out = _dot(_silu(hid).astype(BF16), w2_ref[...].astype(BF16))
    if is_key:
        out = _norm_rope(out, g_ref[...], c_ref[...], sa_ref[...], sb_ref[...])
    o_ref[0] = out.astype(o_ref.dtype)


def compress(t16, pe, w1, w2, g, tables, is_key):
    ng, n, wid = t16.shape
    tab_spec = pl.BlockSpec((n, HEAD_DIM), lambda i: (0, 0))
    return pl.pallas_call(
        functools.partial(_compress_kernel, is_key=is_key),
        grid=(ng,),
        in_specs=[pl.BlockSpec((1, n, wid), lambda i: (i, 0, 0)),
                  pl.BlockSpec((1, 2 * wid), lambda i: (0, 0)),
                  pl.BlockSpec((2 * wid, HEAD_DIM), lambda i: (0, 0)),
                  pl.BlockSpec((HEAD_DIM, HEAD_DIM), lambda i: (0, 0)),
                  pl.BlockSpec((1, HEAD_DIM), lambda i: (0, 0)),
                  tab_spec, tab_spec, tab_spec],
        out_specs=pl.BlockSpec((1, n, HEAD_DIM), lambda i: (i, 0, 0)),
        out_shape=jax.ShapeDtypeStruct((ng, n, HEAD_DIM), BF16),
        compiler_params=_params(("parallel",)),
        name="compress_k" if is_key else "compress_v",
    )(t16, pe.reshape(1, 2 * wid), w1, w2, g.reshape(1, HEAD_DIM), *tables)


def _conv_silu(x, carry_ref, w_ref, b_ref, first):
    @pl.when(first)
    def _():
        carry_ref[...] = jnp.zeros_like(carry_ref)

    full = jnp.concatenate([carry_ref[...], x], axis=0)
    n = x.shape[0]
    acc = b_ref[...] + w_ref[CONV_WIDTH - 1:CONV_WIDTH, :] * x
    for k in range(1, CONV_WIDTH):
        acc = acc + w_ref[CONV_WIDTH - 1 - k:CONV_WIDTH - k, :] * full[8 - k:8 - k + n]
    carry_ref[...] = x[n - 8:, :]
    return _silu(acc)


def _ssd_kernel(z_ref, x_ref, b_ref, c_ref, dt_ref, dtt_ref,
                cwx_ref, cwb_ref, cwc_ref, cbx_ref, cbb_ref, cbc_ref,
                bias_ref, biast_ref, alog_ref, alogt_ref, d_ref, ng_ref, spread_ref,
                o_ref,
                state_ref, xs_ref, y_ref, dtx_ref, ecx_ref, tex_ref, cx_ref, cb_ref, cc_ref):
    ci = pl.program_id(1)
    first = ci == 0
    L = CHUNK

    @pl.when(first)
    def _():
        state_ref[...] = jnp.zeros_like(state_ref)

    xg = _conv_silu(x_ref[...], cx_ref, cwx_ref, cbx_ref, first)
    bm = _conv_silu(b_ref[...], cb_ref, cwb_ref, cbb_ref, first)
    cm = _conv_silu(c_ref[...], cc_ref, cwc_ref, cbc_ref, first)

    dt = _softplus(dt_ref[0] + bias_ref[0])
    dtt = _softplus(dtt_ref[0] + biast_ref[0])
    a = dt * (-jnp.exp(alog_ref[0]))
    at = dtt * (-jnp.exp(alogt_ref[0]))

    row = lax.broadcasted_iota(jnp.int32, (L, L), 0)
    col = lax.broadcasted_iota(jnp.int32, (L, L), 1)
    causal = row >= col
    tril = causal.astype(BF16)
    triu = (row <= col).astype(BF16)
    cum = sum(_dot(tril, p) for p in _split3(a))
    cumt = sum(_dot(p, triu) for p in _split3(at))
    ecum = jnp.exp(cum)
    last = cum[L - 1:L, :]
    toend = jnp.exp(last - cum)
    elast = jnp.exp(last)

    bmb = bm.astype(BF16)
    cmb = cm.astype(BF16)
    cbmat = _dot_nt(cmb, bmb)
    yoff = _dot(cmb, state_ref[...].astype(BF16))

    spread = spread_ref[...]

    def per_channel(v):
        return sum(_dot(piece, spread) for piece in _split3(v))

    dtx_ref[...] = per_channel(dt)
    ecx_ref[...] = per_channel(ecum)
    tex_ref[...] = per_channel(toend)

    lane = lax.broadcasted_iota(jnp.int32, (L, LANES), 1)
    lo = lane < SSM_HEAD_DIM
    lane1 = lax.broadcasted_iota(jnp.int32, (1, LANES), 1)
    lo1 = lane1 < SSM_HEAD_DIM
    for k in range(SSM_HPG // 2):
        h0, h1 = 2 * k, 2 * k + 1
        sl = slice(k * LANES, (k + 1) * LANES)
        xp = xg[:, sl]
        xdt = xp * dtx_ref[:, sl]
        g0 = (cbmat * jnp.exp(jnp.where(causal, cum[:, h0:h0 + 1] - cumt[h0:h0 + 1, :], -jnp.inf))).astype(BF16)
        g1 = (cbmat * jnp.exp(jnp.where(causal, cum[:, h1:h1 + 1] - cumt[h1:h1 + 1, :], -jnp.inf))).astype(BF16)
        ydiag = (_dot(g0, jnp.where(lo, xdt, 0.0).astype(BF16))
                 + _dot(g1, jnp.where(lo, 0.0, xdt).astype(BF16)))
        y = ydiag + yoff[:, sl] * ecx_ref[:, sl] + d_ref[:, sl] * xp
        y_ref[:, sl] = y * _silu(z_ref[:, sl])
        xs_ref[:, sl] = (xdt * tex_ref[:, sl]).astype(BF16)
        state_ref[:, sl] = state_ref[:, sl] * jnp.where(lo1, elast[:, h0:h0 + 1], elast[:, h1:h1 + 1])

    state_ref[...] += _dot_tn(bmb, xs_ref[...])

    y = y_ref[...]
    ms = jnp.mean(y * y, axis=-1, keepdims=True)
    o_ref[...] = (y * lax.rsqrt(ms + EPS) * ng_ref[...]).astype(o_ref.dtype)


def ssd_mixer_core(zx, bcdt, conv_w, conv_b, dt_bias, a_log, d_skip, norm_g):
    s = zx.shape[0]
    nc = s // CHUNK
    gw, n, hpg, ng = SSM_GROUP_W, SSM_STATE, SSM_HPG, SSM_GROUPS
    dt_raw = bcdt[:, 2 * ng * n:2 * ng * n + ng * hpg]
    pad = ((0, 0), (0, 0), (0, LANES - hpg))
    dt_g = jnp.pad(dt_raw.reshape(s, ng, hpg).transpose(1, 0, 2), pad)
    dt_gt = dt_raw.reshape(s, ng, hpg).transpose(1, 2, 0)

    def per_head(v):
        return jnp.pad(v.reshape(ng, 1, hpg), pad), v.reshape(ng, hpg, 1)

    bias, biast = per_head(dt_bias)
    alog, alogt = per_head(a_log)
    d_ch = jnp.repeat(d_skip, SSM_HEAD_DIM).reshape(1, D_INNER)
    spread = jnp.asarray(np.kron(np.eye(LANES, hpg), np.ones((1, SSM_HEAD_DIM))), BF16)
    cb2 = conv_b.reshape(1, -1)
    xoff, boff, coff = D_INNER // gw, D_INNER // n, D_INNER // n + ng
    head_spec = pl.BlockSpec((1, 1, LANES), lambda g, c: (g, 0, 0))
    headt_spec = pl.BlockSpec((1, hpg, 1), lambda g, c: (g, 0, 0))
    return pl.pallas_call(
        _ssd_kernel,
        grid=(ng, nc),
        in_specs=[
            pl.BlockSpec((CHUNK, gw), lambda g, c: (c, g)),
            pl.BlockSpec((CHUNK, gw), lambda g, c: (c, xoff + g)),
            pl.BlockSpec((CHUNK, n), lambda g, c: (c, g)),
            pl.BlockSpec((CHUNK, n), lambda g, c: (c, ng + g)),
            pl.BlockSpec((1, CHUNK, LANES), lambda g, c: (g, c, 0)),
            pl.BlockSpec((1, hpg, CHUNK), lambda g, c: (g, 0, c)),
            pl.BlockSpec((CONV_WIDTH, gw), lambda g, c: (0, g)),
            pl.BlockSpec((CONV_WIDTH, n), lambda g, c: (0, boff + g)),
            pl.BlockSpec((CONV_WIDTH, n), lambda g, c: (0, coff + g)),
            pl.BlockSpec((1, gw), lambda g, c: (0, g)),
            pl.BlockSpec((1, n), lambda g, c: (0, boff + g)),
            pl.BlockSpec((1, n), lambda g, c: (0, coff + g)),
            head_spec, headt_spec, head_spec, headt_spec,
            pl.BlockSpec((1, gw), lambda g, c: (0, g)),
            pl.BlockSpec((1, gw), lambda g, c: (0, g)),
            pl.BlockSpec((LANES, gw), lambda g, c: (0, 0)),
        ],
        out_specs=pl.BlockSpec((CHUNK, gw), lambda g, c: (c, g)),
        out_shape=jax.ShapeDtypeStruct((s, D_INNER), BF16),
        scratch_shapes=[
            pltpu.VMEM((n, gw), F32),
            pltpu.VMEM((CHUNK, gw), BF16),
            pltpu.VMEM((CHUNK, gw), F32),
            pltpu.VMEM((CHUNK, gw), F32),
            pltpu.VMEM((CHUNK, gw), F32),
            pltpu.VMEM((CHUNK, gw), F32),
            pltpu.VMEM((8, gw), F32),
            pltpu.VMEM((8, n), F32),
            pltpu.VMEM((8, n), F32),
        ],
        compiler_params=_params(("parallel", "arbitrary")),
        name="ssd",
    )(zx, zx, bcdt, bcdt, dt_g, dt_gt, conv_w, conv_w, conv_w, cb2, cb2, cb2,
      bias, biast, alog, alogt, d_ch, norm_g.reshape(1, -1), spread)


SEL_TILE = 256
SEL_SHIFT = SEL_BLOCK.bit_length() - 1
WIN_TILES = WINDOW // Q_BLOCK + 1
WIN_KEYS = WIN_TILES * Q_BLOCK
SUM_FLOOR = 1e-30


def _selection_matrix(n_cmp_pad, n_sblk):
    ratio = SEL_BLOCK // CMP_STRIDE
    offs = np.arange(-(CMP_BLOCK // CMP_STRIDE), ratio + 1)
    ov = np.clip(np.minimum(offs * CMP_STRIDE + CMP_BLOCK, SEL_BLOCK) - np.maximum(offs * CMP_STRIDE, 0), 0, None)
    m = np.zeros((n_cmp_pad, n_sblk), np.float32)
    for o, w in zip(offs, ov):
        if w > 0:
            for j in range(n_sblk):
                c = j * ratio + o
                if 0 <= c < n_cmp_pad - 1:
                    m[c, j] = w / CMP_BLOCK
    return m


def _rows(r):
    return slice(r * Q_BLOCK, (r + 1) * Q_BLOCK)


def _lane_fold(x, op):
    out = x[:, :LANES]
    for f in range(1, x.shape[1] // LANES):
        out = op(out, x[:, f * LANES:(f + 1) * LANES])
    return out


def _tile_lanes(x, width):
    return jnp.concatenate([x] * (width // LANES), axis=1)


def _row_stat(x, op, reduce):
    return jnp.broadcast_to(reduce(_lane_fold(x, op), axis=-1, keepdims=True), (x.shape[0], LANES))


def _nsa_kernel(q_ref, gate_ref, kc_ref, vc_ref, ks_ref, vs_ref, kw_ref, vw_ref, selmt_ref, shift_ref,
                gspread_ref, o_ref,
                gexp_ref, bias_ref, q_scr, s_ref, p_ref, s2_ref, p2_ref, b2_ref, psum_ref, stm_ref, stl_ref, wl_ref,
                mx_ref, lacc_ref, acc_ref, oc_ref, ow_ref):
    qb = pl.program_id(1)
    t0 = qb * Q_BLOCK
    R, QB, D = Q_PER_KV, Q_BLOCK, HEAD_DIM
    n_cmp = kc_ref.shape[1]
    n_sblk = selmt_ref.shape[0]

    tok = t0 + lax.broadcasted_iota(jnp.int32, (QB, 1), 0)
    for r in range(R):
        q_scr[_rows(r), :] = q_ref[:, r * D:(r + 1) * D]

    cend = lax.broadcasted_iota(jnp.int32, (QB, n_cmp), 1) * CMP_STRIDE + (CMP_BLOCK - 1)
    b2_ref[:, :n_cmp] = jnp.where(cend <= tok, 0.0, NEG)
    s_ref[:, :n_cmp] = _dot_nt(q_scr[...], kc_ref[0])
    for r in range(R):
        stm_ref[_rows(r)] = _row_stat(s_ref[_rows(r), :n_cmp] + b2_ref[:, :n_cmp], jnp.maximum, jnp.max)
    for r in range(R):
        e = jnp.exp(s_ref[_rows(r), :n_cmp] + b2_ref[:, :n_cmp] - _tile_lanes(stm_ref[_rows(r)], n_cmp))
        s_ref[_rows(r), :n_cmp] = e
        stl_ref[_rows(r)] = _row_stat(e, jnp.add, jnp.sum)
    has_key = tok >= CMP_BLOCK - 1
    psum_ref[...] = jnp.zeros(psum_ref.shape, F32)
    for r in range(R):
        inv = jnp.where(has_key, 1.0 / stl_ref[_rows(r)], 0.0)
        p = s_ref[_rows(r), :n_cmp] * _tile_lanes(inv, n_cmp)
        psum_ref[...] += p
        p_ref[_rows(r), :n_cmp] = p.astype(BF16)
    oc_ref[...] = _dot(p_ref[:, :n_cmp], vc_ref[0])
    psum = psum_ref[...]

    selmt = selmt_ref[...].astype(BF16)
    imp = sum(_dot_nt(selmt, piece) for piece in _split3(psum))
    jj = lax.broadcasted_iota(jnp.int32, (n_sblk, QB), 0)
    jf = jj.astype(F32)
    tok_t = t0 + lax.broadcasted_iota(jnp.int32, (1, QB), 1)
    cur = jnp.right_shift(tok_t, SEL_SHIFT)
    forced = (jj == 0) | (jj == cur) | (jj == cur - 1)
    score = jnp.where(jj <= cur, imp + jnp.where(forced, FORCE_BONUS, 0.0), NEG)
    sel_t = jnp.zeros((n_sblk, QB), F32)
    for _ in range(min(N_SELECT, n_sblk)):
        mx = jnp.max(score, axis=0, keepdims=True)
        idx = jnp.min(jnp.where(score == mx, jf, float(n_sblk)), axis=0, keepdims=True)
        hit = jf == idx
        sel_t = jnp.where(hit, 1.0, sel_t)
        score = jnp.where(hit, -jnp.inf, score)
    selb = sel_t.T.astype(BF16)

    kws, vws = [], []
    for i in range(WIN_TILES):
        tile = qb - (WIN_TILES - 1) + i
        k0 = pl.multiple_of(jnp.maximum(tile, 0) * QB, QB)
        kws.append(kw_ref[pl.ds(k0, QB), :])
        vws.append(vw_ref[pl.ds(k0, QB), :])
        spos = tile * QB + lax.broadcasted_iota(jnp.int32, (QB, QB), 1)
        ok = (spos <= tok) & (spos > tok - WINDOW) & (spos >= 0)
        b2_ref[:, i * QB:(i + 1) * QB] = jnp.where(ok, 0.0, NEG)
    s_ref[:, :WIN_KEYS] = _dot_nt(q_scr[...], jnp.concatenate(kws, axis=0))
    for r in range(R):
        stm_ref[_rows(r)] = _row_stat(s_ref[_rows(r), :WIN_KEYS] + b2_ref[:, :WIN_KEYS], jnp.maximum, jnp.max)
    for r in range(R):
        e = jnp.exp(s_ref[_rows(r), :WIN_KEYS] + b2_ref[:, :WIN_KEYS] - _tile_lanes(stm_ref[_rows(r)], WIN_KEYS))
        p_ref[_rows(r), :WIN_KEYS] = e.astype(BF16)
        wl_ref[_rows(r)] = _row_stat(e, jnp.add, jnp.sum)
    ow_ref[...] = _dot(p_ref[:, :WIN_KEYS], jnp.concatenate(vws, axis=0))

    T = SEL_TILE
    n_pairs = (t0 + QB + 2 * T - 1) // (2 * T)
    last_tile = ks_ref.shape[0] // T - 1
    shift = shift_ref[...]

    def tile_start(kt):
        return pl.multiple_of(jnp.minimum(kt, last_tile) * T, T)

    def scores(kt):
        return _dot_nt(q_scr[...], ks_ref[pl.ds(tile_start(kt), T), :])

    def build_bias(kt):
        k0 = kt * T
        kpos = k0 + lax.broadcasted_iota(jnp.int32, (n_sblk, T), 1)
        blk = lax.broadcasted_iota(jnp.int32, (n_sblk, T), 0)
        expand = jnp.where(jnp.right_shift(kpos, SEL_SHIFT) == blk, 1.0, 0.0).astype(BF16)
        picked = _dot(selb, expand)
        kq = k0 + lax.broadcasted_iota(jnp.int32, (QB, T), 1)
        bias_ref[kt] = jnp.where((picked > 0.5) & (kq <= tok), -shift, NEG)

    def exp_tile(buf, pbuf, kt, use_max):
        for r in range(R):
            x = buf[_rows(r), :T] + bias_ref[kt]
            if use_max:
                x = x - _tile_lanes(mx_ref[_rows(r)], T)
            e = jnp.exp(x)
            lacc_ref[_rows(r)] += _lane_fold(e, jnp.add)
            pbuf[_rows(r), :T] = e.astype(BF16)

    def sum_pass(use_max, build):
        lacc_ref[...] = jnp.zeros(lacc_ref.shape, F32)
        acc_ref[...] = jnp.zeros(acc_ref.shape, F32)

        def body(j, c):
            a = 2 * j
            s2_ref[...] = scores(a + 1)
            if build:
                build_bias(a + 2)
                build_bias(a + 3)
            exp_tile(s_ref, p_ref, a, use_max)
            acc_ref[...] += _dot(p_ref[:, :T], vs_ref[pl.ds(tile_start(a), T), :])
            s_ref[:, :T] = scores(a + 2)
            exp_tile(s2_ref, p2_ref, a + 1, use_max)
            acc_ref[...] += _dot(p2_ref[...], vs_ref[pl.ds(tile_start(a + 1), T), :])
            return c

        s_ref[:, :T] = scores(0)
        if build:
            build_bias(0)
            build_bias(1)
        lax.fori_loop(0, n_pairs, body, 0)

    sum_pass(use_max=False, build=True)
    l_min = jnp.min(jnp.sum(lacc_ref[...], axis=-1, keepdims=True))

    @pl.when(l_min < SUM_FLOOR)
    def _():
        mx_ref[...] = jnp.full(mx_ref.shape, NEG, F32)

        def fold_max(buf, kt):
            for r in range(R):
                m = _lane_fold(buf[_rows(r), :T] + bias_ref[kt], jnp.maximum)
                mx_ref[_rows(r)] = jnp.maximum(mx_ref[_rows(r)], m)

        def max_body(j, c):
            a = 2 * j
            s2_ref[...] = scores(a + 1)
            fold_max(s_ref, a)
            s_ref[:, :T] = scores(a + 2)
            fold_max(s2_ref, a + 1)
            return c

        s_ref[:, :T] = scores(0)
        lax.fori_loop(0, n_pairs, max_body, 0)
        for r in range(R):
            mx_ref[_rows(r)] = jnp.broadcast_to(jnp.max(mx_ref[_rows(r)], axis=-1, keepdims=True), (QB, LANES))
        sum_pass(use_max=True, build=False)

    gspread = gspread_ref[...]
    gexp_ref[...] = sum(_dot(piece, gspread) for piece in _split3(gate_ref[...]))
    for r in range(R):
        osel = acc_ref[_rows(r)] * (1.0 / jnp.sum(lacc_ref[_rows(r)], axis=-1, keepdims=True))
        ow = ow_ref[_rows(r)] * (1.0 / wl_ref[_rows(r)])
        gc, gs, gw = (gexp_ref[:, (3 * r + b) * LANES:(3 * r + b + 1) * LANES] for b in range(3))
        o = gc * oc_ref[_rows(r)] + gs * osel + gw * ow
        o_ref[:, r * D:(r + 1) * D] = o.astype(o_ref.dtype)


def nsa_attention(q, gates, kc, vc, ks, vs, kw, vw, score_bound):
    s = q.shape[0]
    shift = jnp.full((1, SEL_TILE), score_bound, F32)
    n_gate = 3 * Q_PER_KV
    gspread = jnp.asarray(np.kron(np.eye(LANES, n_gate), np.ones((1, LANES))), BF16)
    nqb = s // Q_BLOCK
    n_cmp = kc.shape[1]
    n_sblk = s // SEL_BLOCK
    gw = Q_PER_KV * HEAD_DIM
    selmt = jnp.asarray(_selection_matrix(n_cmp, n_sblk).T)
    kv_spec = pl.BlockSpec((s, HEAD_DIM), lambda g, i: (0, g))
    cmp_spec = pl.BlockSpec((1, n_cmp, HEAD_DIM), lambda g, i: (g, 0, 0))
    rows = Q_PER_KV * Q_BLOCK
    wide = max(WIN_KEYS, n_cmp, SEL_TILE)
    return pl.pallas_call(
        _nsa_kernel,
        grid=(KV_GROUPS, nqb),
        in_specs=[pl.BlockSpec((Q_BLOCK, gw), lambda g, i: (i, g)),
                  pl.BlockSpec((Q_BLOCK, LANES), lambda g, i: (i, g)),
                  cmp_spec, cmp_spec, kv_spec, kv_spec, kv_spec, kv_spec,
                  pl.BlockSpec((n_sblk, n_cmp), lambda g, i: (0, 0)),
                  pl.BlockSpec((1, SEL_TILE), lambda g, i: (0, 0)),
                  pl.BlockSpec((LANES, n_gate * LANES), lambda g, i: (0, 0))],
        out_specs=pl.BlockSpec((Q_BLOCK, gw), lambda g, i: (i, g)),
        out_shape=jax.ShapeDtypeStruct((s, ATTN_HEADS * HEAD_DIM), BF16),
        scratch_shapes=[
            pltpu.VMEM((Q_BLOCK, n_gate * LANES), F32),
            pltpu.VMEM((s // SEL_TILE + 2, Q_BLOCK, SEL_TILE), F32),
            pltpu.VMEM((rows, HEAD_DIM), BF16),
            pltpu.VMEM((rows, wide), F32),
            pltpu.VMEM((rows, wide), BF16),
            pltpu.VMEM((rows, SEL_TILE), F32),
            pltpu.VMEM((rows, SEL_TILE), BF16),
            pltpu.VMEM((Q_BLOCK, wide), F32),
            pltpu.VMEM((Q_BLOCK, n_cmp), F32),
            pltpu.VMEM((rows, LANES), F32),
            pltpu.VMEM((rows, LANES), F32),
            pltpu.VMEM((rows, LANES), F32),
            pltpu.VMEM((rows, LANES), F32),
            pltpu.VMEM((rows, LANES), F32),
            pltpu.VMEM((rows, HEAD_DIM), F32),
            pltpu.VMEM((rows, HEAD_DIM), F32),
            pltpu.VMEM((rows, HEAD_DIM), F32),
        ],
        compiler_params=_params(("parallel", "arbitrary")),
        name="nsa_attention",
    )(q, gates, kc, vc, ks, vs, kw, vw, selmt, shift, gspread)


def _ffn(h, norm_g, w_in, w_out, layer):
    act, w_out_bf16 = ffn_in(rmsnorm(h, norm_g[layer]), w_in, w_out, layer)
    return mm_res_bf16(act, w_out_bf16, h, 0.5)


def _mamba(h, norm_g, w_in, conv_w, conv_b, dt_bias, a_log, d_skip, ssm_norm, w_out, layer):
    u = rmsnorm(h, norm_g)
    zx_cols = 2 * D_INNER
    zx = mm_plain(u, w_in, layer, zx_cols, 512)
    w_tail = w_in[layer, :, zx_cols:]
    w_tail = jnp.pad(w_tail, ((0, 0), (0, -w_tail.shape[1] % 256)))
    bcdt = mm_plain(u, w_tail, 0, w_tail.shape[1], 256)
    y = ssd_mixer_core(zx, bcdt, conv_w[layer], conv_b[layer], dt_bias[layer], a_log[layer], d_skip[layer],
                       ssm_norm[layer])
    return mm_res(y, w_out, layer, h, 1.0)


def _shared_kv(h, kv_norm, kv_w, cmp_pe_k, cmp_w1_k, cmp_w2_k, cmp_pe_v, cmp_w1_v, cmp_w2_v,
               k_norm_cmp, k_norm_slc, k_norm_win):
    s = h.shape[0]
    gd = KV_GROUPS * HEAD_DIM
    kv = mm_plain(rmsnorm(h, kv_norm), kv_w, 0, 6 * gd, 512)
    n16 = s // CMP_STRIDE

    def blocks16(t):
        return t.reshape(n16, CMP_STRIDE, KV_GROUPS, HEAD_DIM).transpose(2, 0, 1, 3).reshape(
            KV_GROUPS, n16, CMP_STRIDE * HEAD_DIM)

    cmp_tables = rope_tables(jnp.arange(n16) * CMP_STRIDE + (CMP_BLOCK - 1))
    pos_tables = rope_tables(jnp.arange(s))
    kc = compress(blocks16(kv[:, 0:gd]), cmp_pe_k, cmp_w1_k, cmp_w2_k, k_norm_cmp, cmp_tables, True)
    vc = compress(blocks16(kv[:, gd:2 * gd]), cmp_pe_v, cmp_w1_v, cmp_w2_v, k_norm_cmp, cmp_tables, False)
    ks = head_norm_rope(kv, 2 * gd, KV_GROUPS, k_norm_slc, pos_tables)
    kw = head_norm_rope(kv, 4 * gd, KV_GROUPS, k_norm_win, pos_tables)
    vs = kv[:, 3 * gd:4 * gd].astype(BF16)
    vw = kv[:, 5 * gd:6 * gd].astype(BF16)
    k_bound = math.sqrt(HEAD_DIM) * jnp.max(jnp.abs(k_norm_slc))
    return kc, vc, ks, vs, kw, vw, pos_tables, k_bound


def _nsa(h, norm_g, w_qg, q_norm_g, w_o, shared, layer):
    kc, vc, ks, vs, kw, vw, pos_tables, k_bound = shared
    u = rmsnorm(h, norm_g)
    qd = ATTN_HEADS * HEAD_DIM
    per_g = Q_PER_KV * 3
    w_g = w_qg[layer, :, qd:].reshape(D_MODEL, KV_GROUPS, per_g)
    w_g = jnp.pad(w_g, ((0, 0), (0, 0), (0, LANES - per_g))).reshape(D_MODEL, KV_GROUPS * LANES)
    gates = mm_plain(u, w_g, 0, KV_GROUPS * LANES, KV_GROUPS * LANES, act="sigmoid")
    scale = HEAD_DIM ** -0.5
    qn = proj_norm_rope(u, w_qg, layer, qd, q_norm_g[layer], pos_tables, scale)
    q_bound = math.sqrt(HEAD_DIM) * scale * jnp.max(jnp.abs(q_norm_g[layer]))
    o = nsa_attention(qn, gates, kc, vc, ks, vs, kw, vw, 1.01 * q_bound * k_bound)
    return mm_res(o, w_o, layer, h, 1.0, tn=512)


def kernel(x, ffn_a_norm, ffn_a_w_in, ffn_a_w_out, ffn_b_norm, ffn_b_w_in, ffn_b_w_out, mix_norm,
           ssm_w_in, ssm_conv_w, ssm_conv_b, ssm_dt_bias, ssm_a_log, ssm_d, ssm_norm, ssm_w_out,
           kv_norm, kv_w, cmp_pe_k, cmp_w1_k, cmp_w2_k, cmp_pe_v, cmp_w1_v, cmp_w2_v,
           k_norm_cmp, k_norm_slc, k_norm_win, attn_w_qg, attn_q_norm, attn_w_o):
    b, s, d = x.shape
    n_layers = ffn_a_norm.shape[0]
    n_a = ssm_w_in.shape[0]
    outs = []
    for bi in range(b):
        h = x.reshape(s, d) if b == 1 else x[bi]
        shared = None
        for i in range(n_layers):
            h = _ffn(h, ffn_a_norm, ffn_a_w_in, ffn_a_w_out, i)
            if i < n_a:
                h = _mamba(h, mix_norm[i], ssm_w_in, ssm_conv_w, ssm_conv_b, ssm_dt_bias,
                           ssm_a_log, ssm_d, ssm_norm, ssm_w_out, i)
            else:
                h = _nsa(h, mix_norm[i], attn_w_qg, attn_q_norm, attn_w_o, shared, i - n_a)
            h = _ffn(h, ffn_b_norm, ffn_b_w_in, ffn_b_w_out, i)
            if i == n_a - 1:
                shared = _shared_kv(h, kv_norm, kv_w, cmp_pe_k, cmp_w1_k, cmp_w2_k, cmp_pe_v, cmp_w1_v,
                                    cmp_w2_v, k_norm_cmp, k_norm_slc, k_norm_win)
        outs.append(h)
    return outs[0].reshape(b, s, d) if b == 1 else jnp.stack(outs, axis=0)
```

```python
import functools
import math

import jax
import jax.numpy as jnp
import numpy as np
from jax import lax
from jax.experimental import pallas as pl
from jax.experimental.pallas import tpu as pltpu

F32 = jnp.float32
BF16 = jnp.bfloat16

D_MODEL = 4096
D_FF = 11008
D_INNER = 8192
SSM_HEAD_DIM = 64
SSM_GROUPS = 8
SSM_HPG = 16
SSM_STATE = 128
SSM_GROUP_W = SSM_HPG * SSM_HEAD_DIM
CONV_WIDTH = 4
CHUNK = 128
ATTN_HEADS = 32
KV_GROUPS = 4
Q_PER_KV = 8
HEAD_DIM = 128
CMP_BLOCK = 32
CMP_STRIDE = 16
SEL_BLOCK = 64
N_SELECT = 16
WINDOW = 512
Q_BLOCK = 128
ROPE_THETA = 500000.0
ROPE_DIM = 32
EPS = 1e-6
NEG = -1e30
FORCE_BONUS = 1e4

LANES = 128
VMEM_LIMIT = 56 * 1024 * 1024


def _params(sem):
    return pltpu.CompilerParams(dimension_semantics=sem, vmem_limit_bytes=VMEM_LIMIT)


def _dot(a, b):
    return jnp.dot(a, b, preferred_element_type=F32)


def _dot_nt(a, b):
    return lax.dot_general(a, b, (((1,), (1,)), ((), ())), preferred_element_type=F32)


def _dot_tn(a, b):
    return lax.dot_general(a, b, (((0,), (0,)), ((), ())), preferred_element_type=F32)


def _split3(x):
    hi = x.astype(BF16)
    r1 = x - hi.astype(F32)
    mid = r1.astype(BF16)
    lo = (r1 - mid.astype(F32)).astype(BF16)
    return hi, mid, lo


def _sigmoid(x):
    return 0.5 * jnp.tanh(0.5 * x) + 0.5


def _silu(x):
    return x * _sigmoid(x)


def _softplus(x):
    return jnp.maximum(x, 0.0) + jnp.log1p(jnp.exp(-jnp.abs(x)))


def _rmsnorm_kernel(x_ref, g_ref, o_ref):
    x = x_ref[...]
    ms = jnp.mean(x * x, axis=-1, keepdims=True)
    o_ref[...] = (x * lax.rsqrt(ms + EPS) * g_ref[...]).astype(o_ref.dtype)


def rmsnorm(x, g, tm=256):
    s, d = x.shape
    return pl.pallas_call(
        _rmsnorm_kernel,
        grid=(s // tm,),
        in_specs=[pl.BlockSpec((tm, d), lambda i: (i, 0)),
                  pl.BlockSpec((1, d), lambda i: (0, 0))],
        out_specs=pl.BlockSpec((tm, d), lambda i: (i, 0)),
        out_shape=jax.ShapeDtypeStruct((s, d), BF16),
        compiler_params=_params(("parallel",)),
        name="rmsnorm",
    )(x, g.reshape(1, d))


def _row_tile(s, want):
    return min(s, want)


def _w_spec(w, layer, k, tn, col0=0):
    if w.ndim == 2:
        return pl.BlockSpec((k, tn), lambda i, j: (0, j + col0))
    return pl.BlockSpec((None, k, tn), lambda i, j: (layer, 0, j + col0))


def _ffn_in_kernel(x_ref, wa_ref, wb_ref, wo_ref, o_ref, wob_ref):
    x = x_ref[...]
    a = _dot(x, wa_ref[...].astype(BF16))
    b = _dot(x, wb_ref[...].astype(BF16))
    o_ref[...] = (_silu(a) * b).astype(o_ref.dtype)
    wob_ref[...] = wo_ref[...].astype(BF16)


def ffn_in(xn, w_in, w_out, layer, tm=1024, tn=256):
    s, k = xn.shape
    f = w_in.shape[-1] // 2
    tm = _row_tile(s, tm)
    nb = f // tn
    n_steps = (s // tm) * nb
    fo, d = w_out.shape[-2:]
    assert fo % n_steps == 0 and (fo // n_steps) % 16 == 0, (fo, n_steps)
    rows = fo // n_steps
    return pl.pallas_call(
        _ffn_in_kernel,
        grid=(s // tm, nb),
        in_specs=[pl.BlockSpec((tm, k), lambda i, j: (i, 0)),
                  _w_spec(w_in, layer, k, tn),
                  _w_spec(w_in, layer, k, tn, nb),
                  pl.BlockSpec((None, rows, d), lambda i, j: (layer, i * nb + j, 0))],
        out_specs=[pl.BlockSpec((tm, tn), lambda i, j: (i, j)),
                   pl.BlockSpec((rows, d), lambda i, j: (i * nb + j, 0))],
        out_shape=[jax.ShapeDtypeStruct((s, f), BF16), jax.ShapeDtypeStruct((fo, d), BF16)],
        compiler_params=_params(("parallel", "arbitrary")),
        name="ffn_in",
    )(xn, w_in, w_in, w_out)


def _mm_res_bf16_kernel(a_ref, w_ref, r_ref, o_ref, *, scale):
    o_ref[...] = r_ref[...] + scale * _dot(a_ref[...], w_ref[...])


def mm_res_bf16(a, w, res, scale, tm=512, tn=512):
    s, k = a.shape
    n = w.shape[1]
    tm = _row_tile(s, tm)
    return pl.pallas_call(
        functools.partial(_mm_res_bf16_kernel, scale=scale),
        grid=(s // tm, n // tn),
        in_specs=[pl.BlockSpec((tm, k), lambda i, j: (i, 0)),
                  pl.BlockSpec((k, tn), lambda i, j: (0, j)),
                  pl.BlockSpec((tm, tn), lambda i, j: (i, j))],
        out_specs=pl.BlockSpec((tm, tn), lambda i, j: (i, j)),
        out_shape=jax.ShapeDtypeStruct((s, n), F32),
        compiler_params=_params(("parallel", "arbitrary")),
        name="mm_res_bf16",
    )(a, w, res)


def _mm_plain_kernel(a_ref, w_ref, *rest, act):
    o_ref = rest[len(rest) // 2]
    acc = _dot(a_ref[...], w_ref[...].astype(BF16))
    if act == "sigmoid":
        acc = _sigmoid(acc)
    o_ref[...] = acc.astype(o_ref.dtype)
    if len(rest) == 3:
        rest[2][...] = rest[0][...].astype(BF16)


def mm_plain(a, w, layer, n, tn, tm=1024, act=None, out_dtype=F32, cast_w=None):
    s, k = a.shape
    tm = _row_tile(s, tm)
    grid = (s // tm, n // tn)
    in_specs = [pl.BlockSpec((tm, k), lambda i, j: (i, 0)), _w_spec(w, layer, k, tn)]
    out_specs = [pl.BlockSpec((tm, tn), lambda i, j: (i, j))]
    out_shape = [jax.ShapeDtypeStruct((s, n), out_dtype)]
    args = [a, w]
    if cast_w is not None:
        k2, n2 = cast_w.shape[-2:]
        n_steps = grid[0] * grid[1]
        assert k2 % n_steps == 0 and (k2 // n_steps) % 16 == 0, (k2, n_steps)
        rows = k2 // n_steps
        in_specs.append(pl.BlockSpec((None, rows, n2), lambda i, j: (layer, i * grid[1] + j, 0)))
        out_specs.append(pl.BlockSpec((rows, n2), lambda i, j: (i * grid[1] + j, 0)))
        out_shape.append(jax.ShapeDtypeStruct((k2, n2), BF16))
        args.append(cast_w)
    res = pl.pallas_call(
        functools.partial(_mm_plain_kernel, act=act),
        grid=grid,
        in_specs=in_specs,
        out_specs=out_specs,
        out_shape=out_shape,
        compiler_params=_params(("parallel", "arbitrary")),
        name="mm_plain",
    )(*args)
    return res if cast_w is not None else res[0]


def rope_tables(pos):
    half = ROPE_DIM // 2
    inv = ROPE_THETA ** (-jnp.arange(half, dtype=F32) / half)
    ang = pos.astype(F32)[:, None] * inv[None, :]
    cos, sin = jnp.cos(ang), jnp.sin(ang)
    n = pos.shape[0]
    ones = jnp.ones((n, HEAD_DIM - ROPE_DIM), F32)
    zeros = jnp.zeros((n, HEAD_DIM - ROPE_DIM), F32)
    z16 = jnp.zeros((n, half), F32)
    cos_t = jnp.concatenate([cos, cos, ones], axis=1)
    sin_a = jnp.concatenate([z16, sin, zeros], axis=1)
    sin_b = jnp.concatenate([-sin, z16, zeros], axis=1)
    return cos_t, sin_a, sin_b


def _norm_rope(x, g, cos_t, sin_a, sin_b):
    ms = jnp.mean(x * x, axis=-1, keepdims=True)
    y = x * lax.rsqrt(ms + EPS) * g
    half = ROPE_DIM // 2
    return (y * cos_t + pltpu.roll(y, half, 1) * sin_a
            + pltpu.roll(y, HEAD_DIM - half, 1) * sin_b)


def _head_norm_rope_kernel(x_ref, g_ref, c_ref, sa_ref, sb_ref, o_ref, *, nh, scale):
    g = g_ref[...]
    c, sa, sb = c_ref[...], sa_ref[...], sb_ref[...]
    for h in range(nh):
        x = x_ref[:, h * HEAD_DIM:(h + 1) * HEAD_DIM]
        y = _norm_rope(x, g, c, sa, sb)
        o_ref[:, h * HEAD_DIM:(h + 1) * HEAD_DIM] = (y * scale).astype(o_ref.dtype)


def head_norm_rope(x, col0, nh, g, tables, scale=1.0, tm=256):
    s = x.shape[0]
    w = nh * HEAD_DIM
    cb = col0 // w
    tm = _row_tile(s, tm)
    tab_spec = pl.BlockSpec((tm, HEAD_DIM), lambda i: (i, 0))
    return pl.pallas_call(
        functools.partial(_head_norm_rope_kernel, nh=nh, scale=scale),
        grid=(s // tm,),
        in_specs=[pl.BlockSpec((tm, w), lambda i: (i, cb)),
                  pl.BlockSpec((1, HEAD_DIM), lambda i: (0, 0)),
                  tab_spec, tab_spec, tab_spec],
        out_specs=pl.BlockSpec((tm, w), lambda i: (i, 0)),
        out_shape=jax.ShapeDtypeStruct((s, w), BF16),
        compiler_params=_params(("parallel",)),
        name="head_norm_rope",
    )(x, g.reshape(1, HEAD_DIM), *tables)


def _compress_kernel(t_ref, pe_ref, w1_ref, w2_ref, g_ref, c_ref, sa_ref, sb_ref, o_ref, *, is_key):
    half = CMP_STRIDE * HEAD_DIM
    t = t_ref[0]
    top = (t + pe_ref[:, :half]).astype(BF16)
    bot = (t + pe_ref[:, half:]).astype(BF16)
    p = _dot(top, w1_ref[:half, :].astype(BF16))
    q = _dot(bot, w1_ref[half:, :].astype(BF16))
    n = t.shape[0]
    hid = p + pltpu.roll(q, n - 1, 0)
    out = _dot(_silu(hid).astype(BF16), w2_ref[...].astype(BF16))
    if is_key:
        out = _norm_rope(out, g_ref[...], c_ref[...], sa_ref[...], sb_ref[...])
    o_ref[0] = out.astype(o_ref.dtype)


def compress(t16, pe, w1, w2, g, tables, is_key):
    ng, n, wid = t16.shape
    tab_spec = pl.BlockSpec((n, HEAD_DIM), lambda i: (0, 0))
    return pl.pallas_call(
        functools.partial(_compress_kernel, is_key=is_key),
        grid=(ng,),
        in_specs=[pl.BlockSpec((1, n, wid), lambda i: (i, 0, 0)),
                  pl.BlockSpec((1, 2 * wid), lambda i: (0, 0)),
                  pl.BlockSpec((2 * wid, HEAD_DIM), lambda i: (0, 0)),
                  pl.BlockSpec((HEAD_DIM, HEAD_DIM), lambda i: (0, 0)),
                  pl.BlockSpec((1, HEAD_DIM), lambda i: (0, 0)),
                  tab_spec, tab_spec, tab_spec],
        out_specs=pl.BlockSpec((1, n, HEAD_DIM), lambda i: (i, 0, 0)),
        out_shape=jax.ShapeDtypeStruct((ng, n, HEAD_DIM), BF16),
        compiler_params=_params(("parallel",)),
        name="compress_k" if is_key else "compress_v",
    )(t16, pe.reshape(1, 2 * wid), w1, w2, g.reshape(1, HEAD_DIM), *tables)


def _conv_silu(x, carry_ref, w_ref, b_ref, first):
    @pl.when(first)
    def _():
        carry_ref[...] = jnp.zeros_like(carry_ref)

    full = jnp.concatenate([carry_ref[...], x], axis=0)
    n = x.shape[0]
    acc = b_ref[...] + w_ref[CONV_WIDTH - 1:CONV_WIDTH, :] * x
    for k in range(1, CONV_WIDTH):
        acc = acc + w_ref[CONV_WIDTH - 1 - k:CONV_WIDTH - k, :] * full[8 - k:8 - k + n]
    carry_ref[...] = x[n - 8:, :]
    return _silu(acc)


def _ssd_kernel(z_ref, x_ref, b_ref, c_ref, dt_ref, dtt_ref,
                cwx_ref, cwb_ref, cwc_ref, cbx_ref, cbb_ref, cbc_ref,
                bias_ref, biast_ref, alog_ref, alogt_ref, d_ref, ng_ref, spread_ref,
                o_ref,
                state_ref, xs_ref, y_ref, dtx_ref, ecx_ref, tex_ref, cx_ref, cb_ref, cc_ref):
    ci = pl.program_id(1)
    first = ci == 0
    L = CHUNK

    @pl.when(first)
    def _():
        state_ref[...] = jnp.zeros_like(state_ref)

    xg = _conv_silu(x_ref[...], cx_ref, cwx_ref, cbx_ref, first)
    bm = _conv_silu(b_ref[...], cb_ref, cwb_ref, cbb_ref, first)
    cm = _conv_silu(c_ref[...], cc_ref, cwc_ref, cbc_ref, first)

    dt = _softplus(dt_ref[0] + bias_ref[0])
    dtt = _softplus(dtt_ref[0] + biast_ref[0])
    a = dt * (-jnp.exp(alog_ref[0]))
    at = dtt * (-jnp.exp(alogt_ref[0]))

    row = lax.broadcasted_iota(jnp.int32, (L, L), 0)
    col = lax.broadcasted_iota(jnp.int32, (L, L), 1)
    causal = row >= col
    tril = causal.astype(BF16)
    triu = (row <= col).astype(BF16)
    cum = sum(_dot(tril, p) for p in _split3(a))
    cumt = sum(_dot(p, triu) for p in _split3(at))
    ecum = jnp.exp(cum)
    last = cum[L - 1:L, :]
    toend = jnp.exp(last - cum)
    elast = jnp.exp(last)

    bmb = bm.astype(BF16)
    cmb = cm.astype(BF16)
    cbmat = _dot_nt(cmb, bmb)
    yoff = _dot(cmb, state_ref[...].astype(BF16))

    spread = spread_ref[...]

    def per_channel(v):
        return sum(_dot(piece, spread) for piece in _split3(v))

    dtx_ref[...] = per_channel(dt)
    ecx_ref[...] = per_channel(ecum)
    tex_ref[...] = per_channel(toend)

    lane = lax.broadcasted_iota(jnp.int32, (L, LANES), 1)
    lo = lane < SSM_HEAD_DIM
    lane1 = lax.broadcasted_iota(jnp.int32, (1, LANES), 1)
    lo1 = lane1 < SSM_HEAD_DIM
    for k in range(SSM_HPG // 2):
        h0, h1 = 2 * k, 2 * k + 1
        sl = slice(k * LANES, (k + 1) * LANES)
        xp = xg[:, sl]
        xdt = xp * dtx_ref[:, sl]
        g0 = (cbmat * jnp.exp(jnp.where(causal, cum[:, h0:h0 + 1] - cumt[h0:h0 + 1, :], -jnp.inf))).astype(BF16)
        g1 = (cbmat * jnp.exp(jnp.where(causal, cum[:, h1:h1 + 1] - cumt[h1:h1 + 1, :], -jnp.inf))).astype(BF16)
        ydiag = (_dot(g0, jnp.where(lo, xdt, 0.0).astype(BF16))
                 + _dot(g1, jnp.where(lo, 0.0, xdt).astype(BF16)))
        y = ydiag + yoff[:, sl] * ecx_ref[:, sl] + d_ref[:, sl] * xp
        y_ref[:, sl] = y * _silu(z_ref[:, sl])
        xs_ref[:, sl] = (xdt * tex_ref[:, sl]).astype(BF16)
        state_ref[:, sl] = state_ref[:, sl] * jnp.where(lo1, elast[:, h0:h0 + 1], elast[:, h1:h1 + 1])

    state_ref[...] += _dot_tn(bmb, xs_ref[...])

    y = y_ref[...]
    ms = jnp.mean(y * y, axis=-1, keepdims=True)
    o_ref[...] = (y * lax.rsqrt(ms + EPS) * ng_ref[...]).astype(o_ref.dtype)


def ssd_mixer_core(zx, bcdt, conv_w, conv_b, dt_bias, a_log, d_skip, norm_g):
    s = zx.shape[0]
    nc = s // CHUNK
    gw, n, hpg, ng = SSM_GROUP_W, SSM_STATE, SSM_HPG, SSM_GROUPS
    dt_raw = bcdt[:, 2 * ng * n:2 * ng * n + ng * hpg]
    pad = ((0, 0), (0, 0), (0, LANES - hpg))
    dt_g = jnp.pad(dt_raw.reshape(s, ng, hpg).transpose(1, 0, 2), pad)
    dt_gt = dt_raw.reshape(s, ng, hpg).transpose(1, 2, 0)

    def per_head(v):
        return jnp.pad(v.reshape(ng, 1, hpg), pad), v.reshape(ng, hpg, 1)

    bias, biast = per_head(dt_bias)
    alog, alogt = per_head(a_log)
    d_ch = jnp.repeat(d_skip, SSM_HEAD_DIM).reshape(1, D_INNER)
    spread = jnp.asarray(np.kron(np.eye(LANES, hpg), np.ones((1, SSM_HEAD_DIM))), BF16)
    cb2 = conv_b.reshape(1, -1)
    xoff, boff, coff = D_INNER // gw, D_INNER // n, D_INNER // n + ng
    head_spec = pl.BlockSpec((1, 1, LANES), lambda g, c: (g, 0, 0))
    headt_spec = pl.BlockSpec((1, hpg, 1), lambda g, c: (g, 0, 0))
    return pl.pallas_call(
        _ssd_kernel,
        grid=(ng, nc),
        in_specs=[
            pl.BlockSpec((CHUNK, gw), lambda g, c: (c, g)),
            pl.BlockSpec((CHUNK, gw), lambda g, c: (c, xoff + g)),
            pl.BlockSpec((CHUNK, n), lambda g, c: (c, g)),
            pl.BlockSpec((CHUNK, n), lambda g, c: (c, ng + g)),
            pl.BlockSpec((1, CHUNK, LANES), lambda g, c: (g, c, 0)),
            pl.BlockSpec((1, hpg, CHUNK), lambda g, c: (g, 0, c)),
            pl.BlockSpec((CONV_WIDTH, gw), lambda g, c: (0, g)),
            pl.BlockSpec((CONV_WIDTH, n), lambda g, c: (0, boff + g)),
            pl.BlockSpec((CONV_WIDTH, n), lambda g, c: (0, coff + g)),
            pl.BlockSpec((1, gw), lambda g, c: (0, g)),
            pl.BlockSpec((1, n), lambda g, c: (0, boff + g)),
            pl.BlockSpec((1, n), lambda g, c: (0, coff + g)),
            head_spec, headt_spec, head_spec, headt_spec,
            pl.BlockSpec((1, gw), lambda g, c: (0, g)),
            pl.BlockSpec((1, gw), lambda g, c: (0, g)),
            pl.BlockSpec((LANES, gw), lambda g, c: (0, 0)),
        ],
        out_specs=pl.BlockSpec((CHUNK, gw), lambda g, c: (c, g)),
        out_shape=jax.ShapeDtypeStruct((s, D_INNER), BF16),
        scratch_shapes=[
            pltpu.VMEM((n, gw), F32),
            pltpu.VMEM((CHUNK, gw), BF16),
            pltpu.VMEM((CHUNK, gw), F32),
            pltpu.VMEM((CHUNK, gw), F32),
            pltpu.VMEM((CHUNK, gw), F32),
            pltpu.VMEM((CHUNK, gw), F32),
            pltpu.VMEM((8, gw), F32),
            pltpu.VMEM((8, n), F32),
            pltpu.VMEM((8, n), F32),
        ],
        compiler_params=_params(("parallel", "arbitrary")),
        name="ssd",
    )(zx, zx, bcdt, bcdt, dt_g, dt_gt, conv_w, conv_w, conv_w, cb2, cb2, cb2,
      bias, biast, alog, alogt, d_ch, norm_g.reshape(1, -1), spread)


SEL_TILE = 256
SEL_SHIFT = SEL_BLOCK.bit_length() - 1
WIN_TILES = WINDOW // Q_BLOCK + 1
WIN_KEYS = WIN_TILES * Q_BLOCK
SUM_FLOOR = 1e-30


def _selection_matrix(n_cmp_pad, n_sblk):
    ratio = SEL_BLOCK // CMP_STRIDE
    offs = np.arange(-(CMP_BLOCK // CMP_STRIDE), ratio + 1)
    ov = np.clip(np.minimum(offs * CMP_STRIDE + CMP_BLOCK, SEL_BLOCK) - np.maximum(offs * CMP_STRIDE, 0), 0, None)
    m = np.zeros((n_cmp_pad, n_sblk), np.float32)
    for o, w in zip(offs, ov):
        if w > 0:
            for j in range(n_sblk):
                c = j * ratio + o
                if 0 <= c < n_cmp_pad - 1:
                    m[c, j] = w / CMP_BLOCK
    return m


def _rows(r):
    return slice(r * Q_BLOCK, (r + 1) * Q_BLOCK)


def _lane_fold(x, op):
    out = x[:, :LANES]
    for f in range(1, x.shape[1] // LANES):
        out = op(out, x[:, f * LANES:(f + 1) * LANES])
    return out


def _tile_lanes(x, width):
    return jnp.concatenate([x] * (width // LANES), axis=1)


def _row_stat(x, op, reduce):
    return jnp.broadcast_to(reduce(_lane_fold(x, op), axis=-1, keepdims=True), (x.shape[0], LANES))


def _nsa_kernel(q_ref, gate_ref, kc_ref, vc_ref, ks_ref, vs_ref, kw_ref, vw_ref, selmt_ref, shift_ref,
                gspread_ref, o_ref,
                gexp_ref, bias_ref, q_scr, s_ref, p_ref, s2_ref, p2_ref, b2_ref, psum_ref, stm_ref, stl_ref, wl_ref,
                mx_ref, lacc_ref, acc_ref, oc_ref, ow_ref):
    qb = pl.program_id(1)
    t0 = qb * Q_BLOCK
    R, QB, D = Q_PER_KV, Q_BLOCK, HEAD_DIM
    n_cmp = kc_ref.shape[1]
    n_sblk = selmt_ref.shape[0]

    tok = t0 + lax.broadcasted_iota(jnp.int32, (QB, 1), 0)
    for r in range(R):
        q_scr[_rows(r), :] = q_ref[:, r * D:(r + 1) * D]

    cend = lax.broadcasted_iota(jnp.int32, (QB, n_cmp), 1) * CMP_STRIDE + (CMP_BLOCK - 1)
    b2_ref[:, :n_cmp] = jnp.where(cend <= tok, 0.0, NEG)
    s_ref[:, :n_cmp] = _dot_nt(q_scr[...], kc_ref[0])
    for r in range(R):
        stm_ref[_rows(r)] = _row_stat(s_ref[_rows(r), :n_cmp] + b2_ref[:, :n_cmp], jnp.maximum, jnp.max)
    for r in range(R):
        e = jnp.exp(s_ref[_rows(r), :n_cmp] + b2_ref[:, :n_cmp] - _tile_lanes(stm_ref[_rows(r)], n_cmp))
        s_ref[_rows(r), :n_cmp] = e
        stl_ref[_rows(r)] = _row_stat(e, jnp.add, jnp.sum)
    has_key = tok >= CMP_BLOCK - 1
    psum_ref[...] = jnp.zeros(psum_ref.shape, F32)
    for r in range(R):
        inv = jnp.where(has_key, 1.0 / stl_ref[_rows(r)], 0.0)
        p = s_ref[_rows(r), :n_cmp] * _tile_lanes(inv, n_cmp)
        psum_ref[...] += p
        p_ref[_rows(r), :n_cmp] = p.astype(BF16)
    oc_ref[...] = _dot(p_ref[:, :n_cmp], vc_ref[0])
    psum = psum_ref[...]

    selmt = selmt_ref[...].astype(BF16)
    imp = sum(_dot_nt(selmt, piece) for piece in _split3(psum))
    jj = lax.broadcasted_iota(jnp.int32, (n_sblk, QB), 0)
    jf = jj.astype(F32)
    tok_t = t0 + lax.broadcasted_iota(jnp.int32, (1, QB), 1)
    cur = jnp.right_shift(tok_t, SEL_SHIFT)
    forced = (jj == 0) | (jj == cur) | (jj == cur - 1)
    score = jnp.where(jj <= cur, imp + jnp.where(forced, FORCE_BONUS, 0.0), NEG)
    sel_t = jnp.zeros((n_sblk, QB), F32)
    for _ in range(min(N_SELECT, n_sblk)):
        mx = jnp.max(score, axis=0, keepdims=True)
        idx = jnp.min(jnp.where(score == mx, jf, float(n_sblk)), axis=0, keepdims=True)
        hit = jf == idx
        sel_t = jnp.where(hit, 1.0, sel_t)
        score = jnp.where(hit, -jnp.inf, score)
    selb = sel_t.T.astype(BF16)

    kws, vws = [], []
    for i in range(WIN_TILES):
        tile = qb - (WIN_TILES - 1) + i
        k0 = pl.multiple_of(jnp.maximum(tile, 0) * QB, QB)
        kws.append(kw_ref[pl.ds(k0, QB), :])
        vws.append(vw_ref[pl.ds(k0, QB), :])
        spos = tile * QB + lax.broadcasted_iota(jnp.int32, (QB, QB), 1)
        ok = (spos <= tok) & (spos > tok - WINDOW) & (spos >= 0)
        b2_ref[:, i * QB:(i + 1) * QB] = jnp.where(ok, 0.0, NEG)
    s_ref[:, :WIN_KEYS] = _dot_nt(q_scr[...], jnp.concatenate(kws, axis=0))
    for r in range(R):
        stm_ref[_rows(r)] = _row_stat(s_ref[_rows(r), :WIN_KEYS] + b2_ref[:, :WIN_KEYS], jnp.maximum, jnp.max)
    for r in range(R):
        e = jnp.exp(s_ref[_rows(r), :WIN_KEYS] + b2_ref[:, :WIN_KEYS] - _tile_lanes(stm_ref[_rows(r)], WIN_KEYS))
        p_ref[_rows(r), :WIN_KEYS] = e.astype(BF16)
        wl_ref[_rows(r)] = _row_stat(e, jnp.add, jnp.sum)
    ow_ref[...] = _dot(p_ref[:, :WIN_KEYS], jnp.concatenate(vws, axis=0))

    T = SEL_TILE
    n_pairs = (t0 + QB + 2 * T - 1) // (2 * T)
    last_tile = ks_ref.shape[0] // T - 1
    shift = shift_ref[...]

    def tile_start(kt):
        return pl.multiple_of(jnp.minimum(kt, last_tile) * T, T)

    def scores(kt):
        return _dot_nt(q_scr[...], ks_ref[pl.ds(tile_start(kt), T), :])

    def build_bias(kt):
        k0 = kt * T
        kpos = k0 + lax.broadcasted_iota(jnp.int32, (n_sblk, T), 1)
        blk = lax.broadcasted_iota(jnp.int32, (n_sblk, T), 0)
        expand = jnp.where(jnp.right_shift(kpos, SEL_SHIFT) == blk, 1.0, 0.0).astype(BF16)
        picked = _dot(selb, expand)
        kq = k0 + lax.broadcasted_iota(jnp.int32, (QB, T), 1)
        bias_ref[kt] = jnp.where((picked > 0.5) & (kq <= tok), -shift, NEG)

    def exp_tile(buf, pbuf, kt, use_max):
        for r in range(R):
            x = buf[_rows(r), :T] + bias_ref[kt]
            if use_max:
                x = x - _tile_lanes(mx_ref[_rows(r)], T)
            e = jnp.exp(x)
            lacc_ref[_rows(r)] += _lane_fold(e, jnp.add)
            pbuf[_rows(r), :T] = e.astype(BF16)

    def sum_pass(use_max, build):
        lacc_ref[...] = jnp.zeros(lacc_ref.shape, F32)
        acc_ref[...] = jnp.zeros(acc_ref.shape, F32)

        def body(j, c):
            a = 2 * j
            s2_ref[...] = scores(a + 1)
            if build:
                build_bias(a + 2)
                build_bias(a + 3)
            exp_tile(s_ref, p_ref, a, use_max)
            acc_ref[...] += _dot(p_ref[:, :T], vs_ref[pl.ds(tile_start(a), T), :])
            s_ref[:, :T] = scores(a + 2)
            exp_tile(s2_ref, p2_ref, a + 1, use_max)
            acc_ref[...] += _dot(p2_ref[...], vs_ref[pl.ds(tile_start(a + 1), T), :])
            return c

        s_ref[:, :T] = scores(0)
        if build:
            build_bias(0)
            build_bias(1)
        lax.fori_loop(0, n_pairs, body, 0)

    sum_pass(use_max=False, build=True)
    l_min = jnp.min(jnp.sum(lacc_ref[...], axis=-1, keepdims=True))

    @pl.when(l_min < SUM_FLOOR)
    def _():
        mx_ref[...] = jnp.full(mx_ref.shape, NEG, F32)

        def fold_max(buf, kt):
            for r in range(R):
                m = _lane_fold(buf[_rows(r), :T] + bias_ref[kt], jnp.maximum)
                mx_ref[_rows(r)] = jnp.maximum(mx_ref[_rows(r)], m)

        def max_body(j, c):
            a = 2 * j
            s2_ref[...] = scores(a + 1)
            fold_max(s_ref, a)
            s_ref[:, :T] = scores(a + 2)
            fold_max(s2_ref, a + 1)
            return c

        s_ref[:, :T] = scores(0)
        lax.fori_loop(0, n_pairs, max_body, 0)
        for r in range(R):
            mx_ref[_rows(r)] = jnp.broadcast_to(jnp.max(mx_ref[_rows(r)], axis=-1, keepdims=True), (QB, LANES))
        sum_pass(use_max=True, build=False)

    gspread = gspread_ref[...]
    gexp_ref[...] = sum(_dot(piece, gspread) for piece in _split3(gate_ref[...]))
    for r in range(R):
        osel = acc_ref[_rows(r)] * (1.0 / jnp.sum(lacc_ref[_rows(r)], axis=-1, keepdims=True))
        ow = ow_ref[_rows(r)] * (1.0 / wl_ref[_rows(r)])
        gc, gs, gw = (gexp_ref[:, (3 * r + b) * LANES:(3 * r + b + 1) * LANES] for b in range(3))
        o = gc * oc_ref[_rows(r)] + gs * osel + gw * ow
        o_ref[:, r * D:(r + 1) * D] = o.astype(o_ref.dtype)


def nsa_attention(q, gates, kc, vc, ks, vs, kw, vw, score_bound):
    s = q.shape[0]
    shift = jnp.full((1, SEL_TILE), score_bound, F32)
    n_gate = 3 * Q_PER_KV
    gspread = jnp.asarray(np.kron(np.eye(LANES, n_gate), np.ones((1, LANES))), BF16)
    nqb = s // Q_BLOCK
    n_cmp = kc.shape[1]
    n_sblk = s // SEL_BLOCK
    gw = Q_PER_KV * HEAD_DIM
    selmt = jnp.asarray(_selection_matrix(n_cmp, n_sblk).T)
    kv_spec = pl.BlockSpec((s, HEAD_DIM), lambda g, i: (0, g))
    cmp_spec = pl.BlockSpec((1, n_cmp, HEAD_DIM), lambda g, i: (g, 0, 0))
    rows = Q_PER_KV * Q_BLOCK
    wide = max(WIN_KEYS, n_cmp, SEL_TILE)
    return pl.pallas_call(
        _nsa_kernel,
        grid=(KV_GROUPS, nqb),
        in_specs=[pl.BlockSpec((Q_BLOCK, gw), lambda g, i: (i, g)),
                  pl.BlockSpec((Q_BLOCK, LANES), lambda g, i: (i, g)),
                  cmp_spec, cmp_spec, kv_spec, kv_spec, kv_spec, kv_spec,
                  pl.BlockSpec((n_sblk, n_cmp), lambda g, i: (0, 0)),
                  pl.BlockSpec((1, SEL_TILE), lambda g, i: (0, 0)),
                  pl.BlockSpec((LANES, n_gate * LANES), lambda g, i: (0, 0))],
        out_specs=pl.BlockSpec((Q_BLOCK, gw), lambda g, i: (i, g)),
        out_shape=jax.ShapeDtypeStruct((s, ATTN_HEADS * HEAD_DIM), BF16),
        scratch_shapes=[
            pltpu.VMEM((Q_BLOCK, n_gate * LANES), F32),
            pltpu.VMEM((s // SEL_TILE + 2, Q_BLOCK, SEL_TILE), F32),
            pltpu.VMEM((rows, HEAD_DIM), BF16),
            pltpu.VMEM((rows, wide), F32),
            pltpu.VMEM((rows, wide), BF16),
            pltpu.VMEM((rows, SEL_TILE), F32),
            pltpu.VMEM((rows, SEL_TILE), BF16),
            pltpu.VMEM((Q_BLOCK, wide), F32),
            pltpu.VMEM((Q_BLOCK, n_cmp), F32),
            pltpu.VMEM((rows, LANES), F32),
            pltpu.VMEM((rows, LANES), F32),
            pltpu.VMEM((rows, LANES), F32),
            pltpu.VMEM((rows, LANES), F32),
            pltpu.VMEM((rows, LANES), F32),
            pltpu.VMEM((rows, HEAD_DIM), F32),
            pltpu.VMEM((rows, HEAD_DIM), F32),
            pltpu.VMEM((rows, HEAD_DIM), F32),
        ],
        compiler_params=_params(("parallel", "arbitrary")),
        name="nsa_attention",
    )(q, gates, kc, vc, ks, vs, kw, vw, selmt, shift, gspread)


def _ffn(h, norm_g, w_in, w_out, layer):
    act, w_out_bf16 = ffn_in(rmsnorm(h, norm_g[layer]), w_in, w_out, layer)
    return mm_res_bf16(act, w_out_bf16, h, 0.5)


def _mamba(h, norm_g, w_in, conv_w, conv_b, dt_bias, a_log, d_skip, ssm_norm, w_out, layer):
    u = rmsnorm(h, norm_g)
    zx_cols = 2 * D_INNER
    zx, w_out_bf16 = mm_plain(u, w_in, layer, zx_cols, 512, cast_w=w_out)
    w_tail = w_in[layer, :, zx_cols:]
    w_tail = jnp.pad(w_tail, ((0, 0), (0, -w_tail.shape[1] % 256)))
    bcdt = mm_plain(u, w_tail, 0, w_tail.shape[1], 256)
    y = ssd_mixer_core(zx, bcdt, conv_w[layer], conv_b[layer], dt_bias[layer], a_log[layer], d_skip[layer],
                       ssm_norm[layer])
    return mm_res_bf16(y, w_out_bf16, h, 1.0)


def _shared_kv(h, kv_norm, kv_w, cmp_pe_k, cmp_w1_k, cmp_w2_k, cmp_pe_v, cmp_w1_v, cmp_w2_v,
               k_norm_cmp, k_norm_slc, k_norm_win):
    s = h.shape[0]
    gd = KV_GROUPS * HEAD_DIM
    kv = mm_plain(rmsnorm(h, kv_norm), kv_w, 0, 6 * gd, 512)
    n16 = s // CMP_STRIDE

    def blocks16(t):
        return t.reshape(n16, CMP_STRIDE, KV_GROUPS, HEAD_DIM).transpose(2, 0, 1, 3).reshape(
            KV_GROUPS, n16, CMP_STRIDE * HEAD_DIM)

    cmp_tables = rope_tables(jnp.arange(n16) * CMP_STRIDE + (CMP_BLOCK - 1))
    pos_tables = rope_tables(jnp.arange(s))
    kc = compress(blocks16(kv[:, 0:gd]), cmp_pe_k, cmp_w1_k, cmp_w2_k, k_norm_cmp, cmp_tables, True)
    vc = compress(blocks16(kv[:, gd:2 * gd]), cmp_pe_v, cmp_w1_v, cmp_w2_v, k_norm_cmp, cmp_tables, False)
    ks = head_norm_rope(kv, 2 * gd, KV_GROUPS, k_norm_slc, pos_tables)
    kw = head_norm_rope(kv, 4 * gd, KV_GROUPS, k_norm_win, pos_tables)
    vs = kv[:, 3 * gd:4 * gd].astype(BF16)
    vw = kv[:, 5 * gd:6 * gd].astype(BF16)
    k_bound = math.sqrt(HEAD_DIM) * jnp.max(jnp.abs(k_norm_slc))
    return kc, vc, ks, vs, kw, vw, pos_tables, k_bound


def _nsa(h, norm_g, w_qg, q_norm_g, w_o, shared, layer):
    kc, vc, ks, vs, kw, vw, pos_tables, k_bound = shared
    u = rmsnorm(h, norm_g)
    qd = ATTN_HEADS * HEAD_DIM
    per_g = Q_PER_KV * 3
    w_g = w_qg[layer, :, qd:].reshape(D_MODEL, KV_GROUPS, per_g)
    w_g = jnp.pad(w_g, ((0, 0), (0, 0), (0, LANES - per_g))).reshape(D_MODEL, KV_GROUPS * LANES)
    gates = mm_plain(u, w_g, 0, KV_GROUPS * LANES, KV_GROUPS * LANES, act="sigmoid")
    scale = HEAD_DIM ** -0.5
    q, w_o_bf16 = mm_plain(u, w_qg, layer, qd, 512, cast_w=w_o)
    qn = head_norm_rope(q, 0, ATTN_HEADS, q_norm_g[layer], pos_tables, scale=scale)
    q_bound = math.sqrt(HEAD_DIM) * scale * jnp.max(jnp.abs(q_norm_g[layer]))
    o = nsa_attention(qn, gates, kc, vc, ks, vs, kw, vw, 1.01 * q_bound * k_bound)
    return mm_res_bf16(o, w_o_bf16, h, 1.0)


def kernel(x, ffn_a_norm, ffn_a_w_in, ffn_a_w_out, ffn_b_norm, ffn_b_w_in, ffn_b_w_out, mix_norm,
           ssm_w_in, ssm_conv_w, ssm_conv_b, ssm_dt_bias, ssm_a_log, ssm_d, ssm_norm, ssm_w_out,
           kv_norm, kv_w, cmp_pe_k, cmp_w1_k, cmp_w2_k, cmp_pe_v, cmp_w1_v, cmp_w2_v,
           k_norm_cmp, k_norm_slc, k_norm_win, attn_w_qg, attn_q_norm, attn_w_o):
    b, s, d = x.shape
    n_layers = ffn_a_norm.shape[0]
    n_a = ssm_w_in.shape[0]
    outs = []
    for bi in range(b):
        h = x.reshape(s, d) if b == 1 else x[bi]
        shared = None
        for i in range(n_layers):
            h = _ffn(h, ffn_a_norm, ffn_a_w_in, ffn_a_w_out, i)
            if i < n_a:
                h = _mamba(h, mix_norm[i], ssm_w_in, ssm_conv_w, ssm_conv_b, ssm_dt_bias,
                           ssm_a_log, ssm_d, ssm_norm, ssm_w_out, i)
            else:
                h = _nsa(h, mix_norm[i], attn_w_qg, attn_q_norm, attn_w_o, shared, i - n_a)
            h = _ffn(h, ffn_b_norm, ffn_b_w_in, ffn_b_w_out, i)
            if i == n_a - 1:
                shared = _shared_kv(h, kv_norm, kv_w, cmp_pe_k, cmp_w1_k, cmp_w2_k, cmp_pe_v, cmp_w1_v,
                                    cmp_w2_v, k_norm_cmp, k_norm_slc, k_norm_win)
        outs.append(h)
    return outs[0].reshape(b, s, d) if b == 1 else jnp.stack(outs, axis=0)
```

```python
import functools
import math

import jax
import jax.numpy as jnp
import numpy as np
from jax import lax
from jax.experimental import pallas as pl
from jax.experimental.pallas import tpu as pltpu

F32 = jnp.float32
BF16 = jnp.bfloat16

D_MODEL = 4096
D_FF = 11008
D_INNER = 8192
SSM_HEAD_DIM = 64
SSM_GROUPS = 8
SSM_HPG = 16
SSM_STATE = 128
SSM_GROUP_W = SSM_HPG * SSM_HEAD_DIM
CONV_WIDTH = 4
CHUNK = 128
ATTN_HEADS = 32
KV_GROUPS = 4
Q_PER_KV = 8
HEAD_DIM = 128
CMP_BLOCK = 32
CMP_STRIDE = 16
SEL_BLOCK = 64
N_SELECT = 16
WINDOW = 512
Q_BLOCK = 128
ROPE_THETA = 500000.0
ROPE_DIM = 32
EPS = 1e-6
NEG = -1e30
FORCE_BONUS = 1e4

LANES = 128
V7X_VMEM_BYTES = 64 * 1024 * 1024
VMEM_LIMIT = V7X_VMEM_BYTES * 7 // 8


def _params(sem):
    return pltpu.CompilerParams(dimension_semantics=sem, vmem_limit_bytes=VMEM_LIMIT)


def _dot(a, b):
    return jnp.dot(a, b, preferred_element_type=F32)


def _dot_nt(a, b):
    return lax.dot_general(a, b, (((1,), (1,)), ((), ())), preferred_element_type=F32)


def _dot_tn(a, b):
    return lax.dot_general(a, b, (((0,), (0,)), ((), ())), preferred_element_type=F32)


def _split3(x):
    hi = x.astype(BF16)
    r1 = x - hi.astype(F32)
    mid = r1.astype(BF16)
    lo = (r1 - mid.astype(F32)).astype(BF16)
    return hi, mid, lo


def _sigmoid(x):
    return 0.5 * jnp.tanh(0.5 * x) + 0.5


def _silu(x):
    return x * _sigmoid(x)


def _softplus(x):
    return jnp.maximum(x, 0.0) + jnp.log1p(jnp.exp(-jnp.abs(x)))


def _rmsnorm_kernel(x_ref, g_ref, o_ref):
    x = x_ref[...]
    ms = jnp.mean(x * x, axis=-1, keepdims=True)
    o_ref[...] = (x * lax.rsqrt(ms + EPS) * g_ref[...]).astype(o_ref.dtype)


def rmsnorm(x, g, tm=256):
    s, d = x.shape
    return pl.pallas_call(
        _rmsnorm_kernel,
        grid=(s // tm,),
        in_specs=[pl.BlockSpec((tm, d), lambda i: (i, 0)),
                  pl.BlockSpec((1, d), lambda i: (0, 0))],
        out_specs=pl.BlockSpec((tm, d), lambda i: (i, 0)),
        out_shape=jax.ShapeDtypeStruct((s, d), BF16),
        compiler_params=_params(("parallel",)),
        name="rmsnorm",
    )(x, g.reshape(1, d))


def _row_tile(s, want):
    return min(s, want)


def _w_spec(w, layer, k, tn, col0=0):
    if w.ndim == 2:
        return pl.BlockSpec((k, tn), lambda i, j: (0, j + col0))
    return pl.BlockSpec((None, k, tn), lambda i, j: (layer, 0, j + col0))


def _ffn_in_kernel(x_ref, wa_ref, wb_ref, wo_ref, o_ref, wob_ref):
    x = x_ref[...]
    a = _dot(x, wa_ref[...].astype(BF16))
    b = _dot(x, wb_ref[...].astype(BF16))
    o_ref[...] = (_silu(a) * b).astype(o_ref.dtype)
    wob_ref[...] = wo_ref[...].astype(BF16)


def ffn_in(xn, w_in, w_out, layer, tm=1024, tn=256):
    s, k = xn.shape
    f = w_in.shape[-1] // 2
    tm = _row_tile(s, tm)
    nb = f // tn
    n_steps = (s // tm) * nb
    fo, d = w_out.shape[-2:]
    assert fo % n_steps == 0 and (fo // n_steps) % 16 == 0, (fo, n_steps)
    rows = fo // n_steps
    return pl.pallas_call(
        _ffn_in_kernel,
        grid=(s // tm, nb),
        in_specs=[pl.BlockSpec((tm, k), lambda i, j: (i, 0)),
                  _w_spec(w_in, layer, k, tn),
                  _w_spec(w_in, layer, k, tn, nb),
                  pl.BlockSpec((None, rows, d), lambda i, j: (layer, i * nb + j, 0))],
        out_specs=[pl.BlockSpec((tm, tn), lambda i, j: (i, j)),
                   pl.BlockSpec((rows, d), lambda i, j: (i * nb + j, 0))],
        out_shape=[jax.ShapeDtypeStruct((s, f), BF16), jax.ShapeDtypeStruct((fo, d), BF16)],
        compiler_params=_params(("parallel", "arbitrary")),
        name="ffn_in",
    )(xn, w_in, w_in, w_out)


def _mm_res_bf16_kernel(a_ref, w_ref, r_ref, o_ref, *, scale):
    o_ref[...] = r_ref[...] + scale * _dot(a_ref[...], w_ref[...])


def mm_res_bf16(a, w, res, scale, tm=512, tn=512):
    s, k = a.shape
    n = w.shape[1]
    tm = _row_tile(s, tm)
    return pl.pallas_call(
        functools.partial(_mm_res_bf16_kernel, scale=scale),
        grid=(s // tm, n // tn),
        in_specs=[pl.BlockSpec((tm, k), lambda i, j: (i, 0)),
                  pl.BlockSpec((k, tn), lambda i, j: (0, j)),
                  pl.BlockSpec((tm, tn), lambda i, j: (i, j))],
        out_specs=pl.BlockSpec((tm, tn), lambda i, j: (i, j)),
        out_shape=jax.ShapeDtypeStruct((s, n), F32),
        compiler_params=_params(("parallel", "arbitrary")),
        name="mm_res_bf16",
    )(a, w, res)


def _mm_plain_kernel(a_ref, w_ref, *rest, act):
    o_ref = rest[len(rest) // 2]
    acc = _dot(a_ref[...], w_ref[...].astype(BF16))
    if act == "sigmoid":
        acc = _sigmoid(acc)
    o_ref[...] = acc.astype(o_ref.dtype)
    if len(rest) == 3:
        rest[2][...] = rest[0][...].astype(BF16)


def mm_plain(a, w, layer, n, tn, tm=1024, act=None, out_dtype=F32, cast_w=None):
    s, k = a.shape
    tm = _row_tile(s, tm)
    grid = (s // tm, n // tn)
    in_specs = [pl.BlockSpec((tm, k), lambda i, j: (i, 0)), _w_spec(w, layer, k, tn)]
    out_specs = [pl.BlockSpec((tm, tn), lambda i, j: (i, j))]
    out_shape = [jax.ShapeDtypeStruct((s, n), out_dtype)]
    args = [a, w]
    if cast_w is not None:
        k2, n2 = cast_w.shape[-2:]
        n_steps = grid[0] * grid[1]
        assert k2 % n_steps == 0 and (k2 // n_steps) % 16 == 0, (k2, n_steps)
        rows = k2 // n_steps
        in_specs.append(pl.BlockSpec((None, rows, n2), lambda i, j: (layer, i * grid[1] + j, 0)))
        out_specs.append(pl.BlockSpec((rows, n2), lambda i, j: (i * grid[1] + j, 0)))
        out_shape.append(jax.ShapeDtypeStruct((k2, n2), BF16))
        args.append(cast_w)
    res = pl.pallas_call(
        functools.partial(_mm_plain_kernel, act=act),
        grid=grid,
        in_specs=in_specs,
        out_specs=out_specs,
        out_shape=out_shape,
        compiler_params=_params(("parallel", "arbitrary")),
        name="mm_plain",
    )(*args)
    return res if cast_w is not None else res[0]


def rope_tables(pos):
    half = ROPE_DIM // 2
    inv = ROPE_THETA ** (-jnp.arange(half, dtype=F32) / half)
    ang = pos.astype(F32)[:, None] * inv[None, :]
    cos, sin = jnp.cos(ang), jnp.sin(ang)
    n = pos.shape[0]
    ones = jnp.ones((n, HEAD_DIM - ROPE_DIM), F32)
    zeros = jnp.zeros((n, HEAD_DIM - ROPE_DIM), F32)
    z16 = jnp.zeros((n, half), F32)
    cos_t = jnp.concatenate([cos, cos, ones], axis=1)
    sin_a = jnp.concatenate([z16, sin, zeros], axis=1)
    sin_b = jnp.concatenate([-sin, z16, zeros], axis=1)
    return cos_t, sin_a, sin_b


def _norm_rope(x, g, cos_t, sin_a, sin_b):
    ms = jnp.mean(x * x, axis=-1, keepdims=True)
    y = x * lax.rsqrt(ms + EPS) * g
    half = ROPE_DIM // 2
    return (y * cos_t + pltpu.roll(y, half, 1) * sin_a
            + pltpu.roll(y, HEAD_DIM - half, 1) * sin_b)


def _head_norm_rope_kernel(x_ref, g_ref, c_ref, sa_ref, sb_ref, o_ref, *, nh, scale):
    g = g_ref[...]
    c, sa, sb = c_ref[...], sa_ref[...], sb_ref[...]
    ones = jnp.ones((HEAD_DIM, HEAD_DIM), BF16)
    half = ROPE_DIM // 2
    for h in range(nh):
        x = x_ref[:, h * HEAD_DIM:(h + 1) * HEAD_DIM]
        ssq = sum(_dot(piece, ones) for piece in _split3(x * x))
        y = x * lax.rsqrt(ssq * (1.0 / HEAD_DIM) + EPS) * g
        y = y * c + pltpu.roll(y, half, 1) * sa + pltpu.roll(y, HEAD_DIM - half, 1) * sb
        o_ref[:, h * HEAD_DIM:(h + 1) * HEAD_DIM] = (y * scale).astype(o_ref.dtype)


def head_norm_rope(x, col0, nh, g, tables, scale=1.0, tm=256):
    s = x.shape[0]
    w = nh * HEAD_DIM
    cb = col0 // w
    tm = _row_tile(s, tm)
    tab_spec = pl.BlockSpec((tm, HEAD_DIM), lambda i: (i, 0))
    return pl.pallas_call(
        functools.partial(_head_norm_rope_kernel, nh=nh, scale=scale),
        grid=(s // tm,),
        in_specs=[pl.BlockSpec((tm, w), lambda i: (i, cb)),
                  pl.BlockSpec((1, HEAD_DIM), lambda i: (0, 0)),
                  tab_spec, tab_spec, tab_spec],
        out_specs=pl.BlockSpec((tm, w), lambda i: (i, 0)),
        out_shape=jax.ShapeDtypeStruct((s, w), BF16),
        compiler_params=_params(("parallel",)),
        name="head_norm_rope",
    )(x, g.reshape(1, HEAD_DIM), *tables)


def _compress_kernel(t_ref, pe_ref, w1_ref, w2_ref, g_ref, c_ref, sa_ref, sb_ref, o_ref, *, is_key):
    half = CMP_STRIDE * HEAD_DIM
    t = t_ref[0]
    top = (t + pe_ref[:, :half]).astype(BF16)
    bot = (t + pe_ref[:, half:]).astype(BF16)
    p = _dot(top, w1_ref[:half, :].astype(BF16))
    q = _dot(bot, w1_ref[half:, :].astype(BF16))
    n = t.shape[0]
    hid = p + pltpu.roll(q, n - 1, 0)
    out = _dot(_silu(hid).astype(BF16), w2_ref[...].astype(BF16))
    if is_key:
        out = _norm_rope(out, g_ref[...], c_ref[...], sa_ref[...], sb_ref[...])
    o_ref[0] = out.astype(o_ref.dtype)


def compress(t16, pe, w1, w2, g, tables, is_key):
    ng, n, wid = t16.shape
    tab_spec = pl.BlockSpec((n, HEAD_DIM), lambda i: (0, 0))
    return pl.pallas_call(
        functools.partial(_compress_kernel, is_key=is_key),
        grid=(ng,),
        in_specs=[pl.BlockSpec((1, n, wid), lambda i: (i, 0, 0)),
                  pl.BlockSpec((1, 2 * wid), lambda i: (0, 0)),
                  pl.BlockSpec((2 * wid, HEAD_DIM), lambda i: (0, 0)),
                  pl.BlockSpec((HEAD_DIM, HEAD_DIM), lambda i: (0, 0)),
                  pl.BlockSpec((1, HEAD_DIM), lambda i: (0, 0)),
                  tab_spec, tab_spec, tab_spec],
        out_specs=pl.BlockSpec((1, n, HEAD_DIM), lambda i: (i, 0, 0)),
        out_shape=jax.ShapeDtypeStruct((ng, n, HEAD_DIM), BF16),
        compiler_params=_params(("parallel",)),
        name="compress_k" if is_key else "compress_v",
    )(t16, pe.reshape(1, 2 * wid), w1, w2, g.reshape(1, HEAD_DIM), *tables)


def _conv_silu(x, carry_ref, w_ref, b_ref, first):
    @pl.when(first)
    def _():
        carry_ref[...] = jnp.zeros_like(carry_ref)

    full = jnp.concatenate([carry_ref[...], x], axis=0)
    n = x.shape[0]
    acc = b_ref[...] + w_ref[CONV_WIDTH - 1:CONV_WIDTH, :] * x
    for k in range(1, CONV_WIDTH):
        acc = acc + w_ref[CONV_WIDTH - 1 - k:CONV_WIDTH - k, :] * full[8 - k:8 - k + n]
    carry_ref[...] = x[n - 8:, :]
    return _silu(acc)


def _ssd_kernel(z_ref, x_ref, b_ref, c_ref, dt_ref, dtt_ref,
                cwx_ref, cwb_ref, cwc_ref, cbx_ref, cbb_ref, cbc_ref,
                bias_ref, biast_ref, alog_ref, alogt_ref, d_ref, ng_ref, spread_ref,
                o_ref,
                state_ref, xs_ref, y_ref, dtx_ref, ecx_ref, tex_ref, cx_ref, cb_ref, cc_ref):
    ci = pl.program_id(1)
    first = ci == 0
    L = CHUNK

    @pl.when(first)
    def _():
        state_ref[...] = jnp.zeros_like(state_ref)

    xg = _conv_silu(x_ref[...], cx_ref, cwx_ref, cbx_ref, first)
    bm = _conv_silu(b_ref[...], cb_ref, cwb_ref, cbb_ref, first)
    cm = _conv_silu(c_ref[...], cc_ref, cwc_ref, cbc_ref, first)

    dt = _softplus(dt_ref[0] + bias_ref[0])
    dtt = _softplus(dtt_ref[0] + biast_ref[0])
    a = dt * (-jnp.exp(alog_ref[0]))
    at = dtt * (-jnp.exp(alogt_ref[0]))

    row = lax.broadcasted_iota(jnp.int32, (L, L), 0)
    col = lax.broadcasted_iota(jnp.int32, (L, L), 1)
    causal = row >= col
    tril = causal.astype(BF16)
    triu = (row <= col).astype(BF16)
    cum = sum(_dot(tril, p) for p in _split3(a))
    cumt = sum(_dot(p, triu) for p in _split3(at))
    ecum = jnp.exp(cum)
    last = cum[L - 1:L, :]
    toend = jnp.exp(last - cum)
    elast = jnp.exp(last)

    bmb = bm.astype(BF16)
    cmb = cm.astype(BF16)
    cbmat = _dot_nt(cmb, bmb)
    yoff = _dot(cmb, state_ref[...].astype(BF16))

    spread = spread_ref[...]

    def per_channel(v):
        return sum(_dot(piece, spread) for piece in _split3(v))

    dtx_ref[...] = per_channel(dt)
    ecx_ref[...] = per_channel(ecum)
    tex_ref[...] = per_channel(toend)

    lane = lax.broadcasted_iota(jnp.int32, (L, LANES), 1)
    lo = lane < SSM_HEAD_DIM
    lane1 = lax.broadcasted_iota(jnp.int32, (1, LANES), 1)
    lo1 = lane1 < SSM_HEAD_DIM
    for k in range(SSM_HPG // 2):
        h0, h1 = 2 * k, 2 * k + 1
        sl = slice(k * LANES, (k + 1) * LANES)
        xp = xg[:, sl]
        xdt = xp * dtx_ref[:, sl]
        g0 = (cbmat * jnp.exp(jnp.where(causal, cum[:, h0:h0 + 1] - cumt[h0:h0 + 1, :], -jnp.inf))).astype(BF16)
        g1 = (cbmat * jnp.exp(jnp.where(causal, cum[:, h1:h1 + 1] - cumt[h1:h1 + 1, :], -jnp.inf))).astype(BF16)
        ydiag = (_dot(g0, jnp.where(lo, xdt, 0.0).astype(BF16))
                 + _dot(g1, jnp.where(lo, 0.0, xdt).astype(BF16)))
        y = ydiag + yoff[:, sl] * ecx_ref[:, sl] + d_ref[:, sl] * xp
        y_ref[:, sl] = y * _silu(z_ref[:, sl])
        xs_ref[:, sl] = (xdt * tex_ref[:, sl]).astype(BF16)
        state_ref[:, sl] = state_ref[:, sl] * jnp.where(lo1, elast[:, h0:h0 + 1], elast[:, h1:h1 + 1])

    state_ref[...] += _dot_tn(bmb, xs_ref[...])

    y = y_ref[...]
    ms = jnp.mean(y * y, axis=-1, keepdims=True)
    o_ref[...] = (y * lax.rsqrt(ms + EPS) * ng_ref[...]).astype(o_ref.dtype)


def ssd_mixer_core(zx, bcdt, conv_w, conv_b, dt_bias, a_log, d_skip, norm_g):
    s = zx.shape[0]
    nc = s // CHUNK
    gw, n, hpg, ng = SSM_GROUP_W, SSM_STATE, SSM_HPG, SSM_GROUPS
    dt_raw = bcdt[:, 2 * ng * n:2 * ng * n + ng * hpg]
    pad = ((0, 0), (0, 0), (0, LANES - hpg))
    dt_g = jnp.pad(dt_raw.reshape(s, ng, hpg).transpose(1, 0, 2), pad)
    dt_gt = dt_raw.reshape(s, ng, hpg).transpose(1, 2, 0)

    def per_head(v):
        return jnp.pad(v.reshape(ng, 1, hpg), pad), v.reshape(ng, hpg, 1)

    bias, biast = per_head(dt_bias)
    alog, alogt = per_head(a_log)
    d_ch = jnp.repeat(d_skip, SSM_HEAD_DIM).reshape(1, D_INNER)
    spread = jnp.asarray(np.kron(np.eye(LANES, hpg), np.ones((1, SSM_HEAD_DIM))), BF16)
    cb2 = conv_b.reshape(1, -1)
    xoff, boff, coff = D_INNER // gw, D_INNER // n, D_INNER // n + ng
    head_spec = pl.BlockSpec((1, 1, LANES), lambda g, c: (g, 0, 0))
    headt_spec = pl.BlockSpec((1, hpg, 1), lambda g, c: (g, 0, 0))
    return pl.pallas_call(
        _ssd_kernel,
        grid=(ng, nc),
        in_specs=[
            pl.BlockSpec((CHUNK, gw), lambda g, c: (c, g)),
            pl.BlockSpec((CHUNK, gw), lambda g, c: (c, xoff + g)),
            pl.BlockSpec((CHUNK, n), lambda g, c: (c, g)),
            pl.BlockSpec((CHUNK, n), lambda g, c: (c, ng + g)),
            pl.BlockSpec((1, CHUNK, LANES), lambda g, c: (g, c, 0)),
            pl.BlockSpec((1, hpg, CHUNK), lambda g, c: (g, 0, c)),
            pl.BlockSpec((CONV_WIDTH, gw), lambda g, c: (0, g)),
            pl.BlockSpec((CONV_WIDTH, n), lambda g, c: (0, boff + g)),
            pl.BlockSpec((CONV_WIDTH, n), lambda g, c: (0, coff + g)),
            pl.BlockSpec((1, gw), lambda g, c: (0, g)),
            pl.BlockSpec((1, n), lambda g, c: (0, boff + g)),
            pl.BlockSpec((1, n), lambda g, c: (0, coff + g)),
            head_spec, headt_spec, head_spec, headt_spec,
            pl.BlockSpec((1, gw), lambda g, c: (0, g)),
            pl.BlockSpec((1, gw), lambda g, c: (0, g)),
            pl.BlockSpec((LANES, gw), lambda g, c: (0, 0)),
        ],
        out_specs=pl.BlockSpec((CHUNK, gw), lambda g, c: (c, g)),
        out_shape=jax.ShapeDtypeStruct((s, D_INNER), BF16),
        scratch_shapes=[
            pltpu.VMEM((n, gw), F32),
            pltpu.VMEM((CHUNK, gw), BF16),
            pltpu.VMEM((CHUNK, gw), F32),
            pltpu.VMEM((CHUNK, gw), F32),
            pltpu.VMEM((CHUNK, gw), F32),
            pltpu.VMEM((CHUNK, gw), F32),
            pltpu.VMEM((8, gw), F32),
            pltpu.VMEM((8, n), F32),
            pltpu.VMEM((8, n), F32),
        ],
        compiler_params=_params(("parallel", "arbitrary")),
        name="ssd",
    )(zx, zx, bcdt, bcdt, dt_g, dt_gt, conv_w, conv_w, conv_w, cb2, cb2, cb2,
      bias, biast, alog, alogt, d_ch, norm_g.reshape(1, -1), spread)


SEL_TILE = 256
SEL_SHIFT = SEL_BLOCK.bit_length() - 1
WIN_TILES = WINDOW // Q_BLOCK + 1
WIN_KEYS = WIN_TILES * Q_BLOCK
SUM_FLOOR = 1e-30


def _selection_matrix(n_cmp_pad, n_sblk):
    ratio = SEL_BLOCK // CMP_STRIDE
    offs = np.arange(-(CMP_BLOCK // CMP_STRIDE), ratio + 1)
    ov = np.clip(np.minimum(offs * CMP_STRIDE + CMP_BLOCK, SEL_BLOCK) - np.maximum(offs * CMP_STRIDE, 0), 0, None)
    m = np.zeros((n_cmp_pad, n_sblk), np.float32)
    for o, w in zip(offs, ov):
        if w > 0:
            for j in range(n_sblk):
                c = j * ratio + o
                if 0 <= c < n_cmp_pad - 1:
                    m[c, j] = w / CMP_BLOCK
    return m


def _rows(r):
    return slice(r * Q_BLOCK, (r + 1) * Q_BLOCK)


def _lane_fold(x, op):
    out = x[:, :LANES]
    for f in range(1, x.shape[1] // LANES):
        out = op(out, x[:, f * LANES:(f + 1) * LANES])
    return out


def _tile_lanes(x, width):
    return jnp.concatenate([x] * (width // LANES), axis=1)


def _row_stat(x, op, reduce):
    return jnp.broadcast_to(reduce(_lane_fold(x, op), axis=-1, keepdims=True), (x.shape[0], LANES))


def _nsa_kernel(q_ref, gate_ref, kc_ref, vc_ref, ks_ref, vs_ref, kw_ref, vw_ref, selmt_ref, shift_ref,
                gspread_ref, o_ref,
                gexp_ref, bias_ref, q_scr, s_ref, p_ref, s2_ref, p2_ref, b2_ref, psum_ref, stm_ref, stl_ref, wl_ref,
                mx_ref, lacc_ref, acc_ref, oc_ref, ow_ref):
    qb = pl.program_id(1)
    t0 = qb * Q_BLOCK
    R, QB, D = Q_PER_KV, Q_BLOCK, HEAD_DIM
    n_cmp = kc_ref.shape[1]
    n_sblk = selmt_ref.shape[0]

    tok = t0 + lax.broadcasted_iota(jnp.int32, (QB, 1), 0)
    for r in range(R):
        q_scr[_rows(r), :] = q_ref[:, r * D:(r + 1) * D]

    cend = lax.broadcasted_iota(jnp.int32, (QB, n_cmp), 1) * CMP_STRIDE + (CMP_BLOCK - 1)
    b2_ref[:, :n_cmp] = jnp.where(cend <= tok, 0.0, NEG)
    s_ref[:, :n_cmp] = _dot_nt(q_scr[...], kc_ref[0])
    for r in range(R):
        stm_ref[_rows(r)] = _row_stat(s_ref[_rows(r), :n_cmp] + b2_ref[:, :n_cmp], jnp.maximum, jnp.max)
    for r in range(R):
        e = jnp.exp(s_ref[_rows(r), :n_cmp] + b2_ref[:, :n_cmp] - _tile_lanes(stm_ref[_rows(r)], n_cmp))
        s_ref[_rows(r), :n_cmp] = e
        stl_ref[_rows(r)] = _row_stat(e, jnp.add, jnp.sum)
    has_key = tok >= CMP_BLOCK - 1
    psum_ref[...] = jnp.zeros(psum_ref.shape, F32)
    for r in range(R):
        inv = jnp.where(has_key, 1.0 / stl_ref[_rows(r)], 0.0)
        p = s_ref[_rows(r), :n_cmp] * _tile_lanes(inv, n_cmp)
        psum_ref[...] += p
        p_ref[_rows(r), :n_cmp] = p.astype(BF16)
    oc_ref[...] = _dot(p_ref[:, :n_cmp], vc_ref[0])
    psum = psum_ref[...]

    selmt = selmt_ref[...].astype(BF16)
    imp = sum(_dot_nt(selmt, piece) for piece in _split3(psum))
    jj = lax.broadcasted_iota(jnp.int32, (n_sblk, QB), 0)
    jf = jj.astype(F32)
    tok_t = t0 + lax.broadcasted_iota(jnp.int32, (1, QB), 1)
    cur = jnp.right_shift(tok_t, SEL_SHIFT)
    forced = (jj == 0) | (jj == cur) | (jj == cur - 1)
    score = jnp.where(jj <= cur, imp + jnp.where(forced, FORCE_BONUS, 0.0), NEG)
    sel_t = jnp.zeros((n_sblk, QB), F32)
    for _ in range(min(N_SELECT, n_sblk)):
        mx = jnp.max(score, axis=0, keepdims=True)
        idx = jnp.min(jnp.where(score == mx, jf, float(n_sblk)), axis=0, keepdims=True)
        hit = jf == idx
        sel_t = jnp.where(hit, 1.0, sel_t)
        score = jnp.where(hit, -jnp.inf, score)
    selb = sel_t.T.astype(BF16)

    kws, vws = [], []
    for i in range(WIN_TILES):
        tile = qb - (WIN_TILES - 1) + i
        k0 = pl.multiple_of(jnp.maximum(tile, 0) * QB, QB)
        kws.append(kw_ref[pl.ds(k0, QB), :])
        vws.append(vw_ref[pl.ds(k0, QB), :])
        spos = tile * QB + lax.broadcasted_iota(jnp.int32, (QB, QB), 1)
        ok = (spos <= tok) & (spos > tok - WINDOW) & (spos >= 0)
        b2_ref[:, i * QB:(i + 1) * QB] = jnp.where(ok, 0.0, NEG)
    s_ref[:, :WIN_KEYS] = _dot_nt(q_scr[...], jnp.concatenate(kws, axis=0))
    for r in range(R):
        stm_ref[_rows(r)] = _row_stat(s_ref[_rows(r), :WIN_KEYS] + b2_ref[:, :WIN_KEYS], jnp.maximum, jnp.max)
    for r in range(R):
        e = jnp.exp(s_ref[_rows(r), :WIN_KEYS] + b2_ref[:, :WIN_KEYS] - _tile_lanes(stm_ref[_rows(r)], WIN_KEYS))
        p_ref[_rows(r), :WIN_KEYS] = e.astype(BF16)
        wl_ref[_rows(r)] = _row_stat(e, jnp.add, jnp.sum)
    ow_ref[...] = _dot(p_ref[:, :WIN_KEYS], jnp.concatenate(vws, axis=0))

    T = SEL_TILE
    n_pairs = (t0 + QB + 2 * T - 1) // (2 * T)
    last_tile = ks_ref.shape[0] // T - 1
    shift = shift_ref[...]

    def tile_start(kt):
        return pl.multiple_of(jnp.minimum(kt, last_tile) * T, T)

    def scores(kt):
        return _dot_nt(q_scr[...], ks_ref[pl.ds(tile_start(kt), T), :])

    def build_bias(kt):
        k0 = kt * T
        kpos = k0 + lax.broadcasted_iota(jnp.int32, (n_sblk, T), 1)
        blk = lax.broadcasted_iota(jnp.int32, (n_sblk, T), 0)
        expand = jnp.where(jnp.right_shift(kpos, SEL_SHIFT) == blk, 1.0, 0.0).astype(BF16)
        picked = _dot(selb, expand)
        kq = k0 + lax.broadcasted_iota(jnp.int32, (QB, T), 1)
        bias_ref[kt] = jnp.where((picked > 0.5) & (kq <= tok), -shift, NEG)

    def exp_tile(buf, pbuf, kt, use_max):
        for r in range(R):
            x = buf[_rows(r), :T] + bias_ref[kt]
            if use_max:
                x = x - _tile_lanes(mx_ref[_rows(r)], T)
            e = jnp.exp(x)
            lacc_ref[_rows(r)] += _lane_fold(e, jnp.add)
            pbuf[_rows(r), :T] = e.astype(BF16)

    def sum_pass(use_max, build):
        lacc_ref[...] = jnp.zeros(lacc_ref.shape, F32)
        acc_ref[...] = jnp.zeros(acc_ref.shape, F32)

        def body(j, c):
            a = 2 * j
            s2_ref[...] = scores(a + 1)
            if build:
                build_bias(a + 2)
                build_bias(a + 3)
            exp_tile(s_ref, p_ref, a, use_max)
            acc_ref[...] += _dot(p_ref[:, :T], vs_ref[pl.ds(tile_start(a), T), :])
            s_ref[:, :T] = scores(a + 2)
            exp_tile(s2_ref, p2_ref, a + 1, use_max)
            acc_ref[...] += _dot(p2_ref[...], vs_ref[pl.ds(tile_start(a + 1), T), :])
            return c

        s_ref[:, :T] = scores(0)
        if build:
            build_bias(0)
            build_bias(1)
        lax.fori_loop(0, n_pairs, body, 0)

    sum_pass(use_max=False, build=True)
    l_min = jnp.min(jnp.sum(lacc_ref[...], axis=-1, keepdims=True))

    @pl.when(l_min < SUM_FLOOR)
    def _():
        mx_ref[...] = jnp.full(mx_ref.shape, NEG, F32)

        def fold_max(buf, kt):
            for r in range(R):
                m = _lane_fold(buf[_rows(r), :T] + bias_ref[kt], jnp.maximum)
                mx_ref[_rows(r)] = jnp.maximum(mx_ref[_rows(r)], m)

        def max_body(j, c):
            a = 2 * j
            s2_ref[...] = scores(a + 1)
            fold_max(s_ref, a)
            s_ref[:, :T] = scores(a + 2)
            fold_max(s2_ref, a + 1)
            return c

        s_ref[:, :T] = scores(0)
        lax.fori_loop(0, n_pairs, max_body, 0)
        for r in range(R):
            mx_ref[_rows(r)] = jnp.broadcast_to(jnp.max(mx_ref[_rows(r)], axis=-1, keepdims=True), (QB, LANES))
        sum_pass(use_max=True, build=False)

    gspread = gspread_ref[...]
    gexp_ref[...] = sum(_dot(piece, gspread) for piece in _split3(gate_ref[...]))
    for r in range(R):
        osel = acc_ref[_rows(r)] * (1.0 / jnp.sum(lacc_ref[_rows(r)], axis=-1, keepdims=True))
        ow = ow_ref[_rows(r)] * (1.0 / wl_ref[_rows(r)])
        gc, gs, gw = (gexp_ref[:, (3 * r + b) * LANES:(3 * r + b + 1) * LANES] for b in range(3))
        o = gc * oc_ref[_rows(r)] + gs * osel + gw * ow
        o_ref[:, r * D:(r + 1) * D] = o.astype(o_ref.dtype)


def nsa_attention(q, gates, kc, vc, ks, vs, kw, vw, score_bound):
    s = q.shape[0]
    shift = jnp.full((1, SEL_TILE), score_bound, F32)
    n_gate = 3 * Q_PER_KV
    gspread = jnp.asarray(np.kron(np.eye(LANES, n_gate), np.ones((1, LANES))), BF16)
    nqb = s // Q_BLOCK
    n_cmp = kc.shape[1]
    n_sblk = s // SEL_BLOCK
    gw = Q_PER_KV * HEAD_DIM
    selmt = jnp.asarray(_selection_matrix(n_cmp, n_sblk).T)
    kv_spec = pl.BlockSpec((s, HEAD_DIM), lambda g, i: (0, g))
    cmp_spec = pl.BlockSpec((1, n_cmp, HEAD_DIM), lambda g, i: (g, 0, 0))
    rows = Q_PER_KV * Q_BLOCK
    wide = max(WIN_KEYS, n_cmp, SEL_TILE)
    return pl.pallas_call(
        _nsa_kernel,
        grid=(KV_GROUPS, nqb),
        in_specs=[pl.BlockSpec((Q_BLOCK, gw), lambda g, i: (i, g)),
                  pl.BlockSpec((Q_BLOCK, LANES), lambda g, i: (i, g)),
                  cmp_spec, cmp_spec, kv_spec, kv_spec, kv_spec, kv_spec,
                  pl.BlockSpec((n_sblk, n_cmp), lambda g, i: (0, 0)),
                  pl.BlockSpec((1, SEL_TILE), lambda g, i: (0, 0)),
                  pl.BlockSpec((LANES, n_gate * LANES), lambda g, i: (0, 0))],
        out_specs=pl.BlockSpec((Q_BLOCK, gw), lambda g, i: (i, g)),
        out_shape=jax.ShapeDtypeStruct((s, ATTN_HEADS * HEAD_DIM), BF16),
        scratch_shapes=[
            pltpu.VMEM((Q_BLOCK, n_gate * LANES), F32),
            pltpu.VMEM((s // SEL_TILE + 2, Q_BLOCK, SEL_TILE), F32),
            pltpu.VMEM((rows, HEAD_DIM), BF16),
            pltpu.VMEM((rows, wide), F32),
            pltpu.VMEM((rows, wide), BF16),
            pltpu.VMEM((rows, SEL_TILE), F32),
            pltpu.VMEM((rows, SEL_TILE), BF16),
            pltpu.VMEM((Q_BLOCK, wide), F32),
            pltpu.VMEM((Q_BLOCK, n_cmp), F32),
            pltpu.VMEM((rows, LANES), F32),
            pltpu.VMEM((rows, LANES), F32),
            pltpu.VMEM((rows, LANES), F32),
            pltpu.VMEM((rows, LANES), F32),
            pltpu.VMEM((rows, LANES), F32),
            pltpu.VMEM((rows, HEAD_DIM), F32),
            pltpu.VMEM((rows, HEAD_DIM), F32),
            pltpu.VMEM((rows, HEAD_DIM), F32),
        ],
        compiler_params=_params(("parallel", "arbitrary")),
        name="nsa_attention",
    )(q, gates, kc, vc, ks, vs, kw, vw, selmt, shift, gspread)


def _ffn(h, norm_g, w_in, w_out, layer):
    act, w_out_bf16 = ffn_in(rmsnorm(h, norm_g[layer]), w_in, w_out, layer)
    return mm_res_bf16(act, w_out_bf16, h, 0.5)


def _mamba(h, norm_g, w_in, conv_w, conv_b, dt_bias, a_log, d_skip, ssm_norm, w_out, layer):
    u = rmsnorm(h, norm_g)
    zx_cols = 2 * D_INNER
    zx, w_out_bf16 = mm_plain(u, w_in, layer, zx_cols, 512, cast_w=w_out)
    w_tail = w_in[layer, :, zx_cols:]
    w_tail = jnp.pad(w_tail, ((0, 0), (0, -w_tail.shape[1] % 256)))
    bcdt = mm_plain(u, w_tail, 0, w_tail.shape[1], 256)
    y = ssd_mixer_core(zx, bcdt, conv_w[layer], conv_b[layer], dt_bias[layer], a_log[layer], d_skip[layer],
                       ssm_norm[layer])
    return mm_res_bf16(y, w_out_bf16, h, 1.0)


def _shared_kv(h, kv_norm, kv_w, cmp_pe_k, cmp_w1_k, cmp_w2_k, cmp_pe_v, cmp_w1_v, cmp_w2_v,
               k_norm_cmp, k_norm_slc, k_norm_win):
    s = h.shape[0]
    gd = KV_GROUPS * HEAD_DIM
    kv = mm_plain(rmsnorm(h, kv_norm), kv_w, 0, 6 * gd, 512)
    n16 = s // CMP_STRIDE

    def blocks16(t):
        return t.reshape(n16, CMP_STRIDE, KV_GROUPS, HEAD_DIM).transpose(2, 0, 1, 3).reshape(
            KV_GROUPS, n16, CMP_STRIDE * HEAD_DIM)

    cmp_tables = rope_tables(jnp.arange(n16) * CMP_STRIDE + (CMP_BLOCK - 1))
    pos_tables = rope_tables(jnp.arange(s))
    kc = compress(blocks16(kv[:, 0:gd]), cmp_pe_k, cmp_w1_k, cmp_w2_k, k_norm_cmp, cmp_tables, True)
    vc = compress(blocks16(kv[:, gd:2 * gd]), cmp_pe_v, cmp_w1_v, cmp_w2_v, k_norm_cmp, cmp_tables, False)
    ks = head_norm_rope(kv, 2 * gd, KV_GROUPS, k_norm_slc, pos_tables)
    kw = head_norm_rope(kv, 4 * gd, KV_GROUPS, k_norm_win, pos_tables)
    vs = kv[:, 3 * gd:4 * gd].astype(BF16)
    vw = kv[:, 5 * gd:6 * gd].astype(BF16)
    k_bound = math.sqrt(HEAD_DIM) * jnp.max(jnp.abs(k_norm_slc))
    return kc, vc, ks, vs, kw, vw, pos_tables, k_bound


def _nsa(h, norm_g, w_qg, q_norm_g, w_o, shared, layer):
    kc, vc, ks, vs, kw, vw, pos_tables, k_bound = shared
    u = rmsnorm(h, norm_g)
    qd = ATTN_HEADS * HEAD_DIM
    per_g = Q_PER_KV * 3
    w_g = w_qg[layer, :, qd:].reshape(D_MODEL, KV_GROUPS, per_g)
    w_g = jnp.pad(w_g, ((0, 0), (0, 0), (0, LANES - per_g))).reshape(D_MODEL, KV_GROUPS * LANES)
    gates = mm_plain(u, w_g, 0, KV_GROUPS * LANES, KV_GROUPS * LANES, act="sigmoid")
    scale = HEAD_DIM ** -0.5
    q, w_o_bf16 = mm_plain(u, w_qg, layer, qd, 512, cast_w=w_o)
    qn = head_norm_rope(q, 0, ATTN_HEADS, q_norm_g[layer], pos_tables, scale=scale)
    q_bound = math.sqrt(HEAD_DIM) * scale * jnp.max(jnp.abs(q_norm_g[layer]))
    o = nsa_attention(qn, gates, kc, vc, ks, vs, kw, vw, 1.01 * q_bound * k_bound)
    return mm_res_bf16(o, w_o_bf16, h, 1.0, tm=1024)


def kernel(x, ffn_a_norm, ffn_a_w_in, ffn_a_w_out, ffn_b_norm, ffn_b_w_in, ffn_b_w_out, mix_norm,
           ssm_w_in, ssm_conv_w, ssm_conv_b, ssm_dt_bias, ssm_a_log, ssm_d, ssm_norm, ssm_w_out,
           kv_norm, kv_w, cmp_pe_k, cmp_w1_k, cmp_w2_k, cmp_pe_v, cmp_w1_v, cmp_w2_v,
           k_norm_cmp, k_norm_slc, k_norm_win, attn_w_qg, attn_q_norm, attn_w_o):
    b, s, d = x.shape
    n_layers = ffn_a_norm.shape[0]
    n_a = ssm_w_in.shape[0]
    outs = []
    for bi in range(b):
        h = x.reshape(s, d) if b == 1 else x[bi]
        shared = None
        for i in range(n_layers):
            h = _ffn(h, ffn_a_norm, ffn_a_w_in, ffn_a_w_out, i)
            if i < n_a:
                h = _mamba(h, mix_norm[i], ssm_w_in, ssm_conv_w, ssm_conv_b, ssm_dt_bias,
                           ssm_a_log, ssm_d, ssm_norm, ssm_w_out, i)
            else:
                h = _nsa(h, mix_norm[i], attn_w_qg, attn_q_norm, attn_w_o, shared, i - n_a)
            h = _ffn(h, ffn_b_norm, ffn_b_w_in, ffn_b_w_out, i)
            if i == n_a - 1:
                shared = _shared_kv(h, kv_norm, kv_w, cmp_pe_k, cmp_w1_k, cmp_w2_k, cmp_pe_v, cmp_w1_v,
                                    cmp_w2_v, k_norm_cmp, k_norm_slc, k_norm_win)
        outs.append(h)
    return outs[0].reshape(b, s, d) if b == 1 else jnp.stack(outs, axis=0)
```

```python
import functools
import math

import jax
import jax.numpy as jnp
import numpy as np
from jax import lax
from jax.experimental import pallas as pl
from jax.experimental.pallas import tpu as pltpu

F32 = jnp.float32
BF16 = jnp.bfloat16

D_MODEL = 4096
D_FF = 11008
D_INNER = 8192
SSM_HEAD_DIM = 64
SSM_GROUPS = 8
SSM_HPG = 16
SSM_STATE = 128
SSM_GROUP_W = SSM_HPG * SSM_HEAD_DIM
CONV_WIDTH = 4
CHUNK = 128
ATTN_HEADS = 32
KV_GROUPS = 4
Q_PER_KV = 8
HEAD_DIM = 128
CMP_BLOCK = 32
CMP_STRIDE = 16
SEL_BLOCK = 64
N_SELECT = 16
WINDOW = 512
Q_BLOCK = 128
ROPE_THETA = 500000.0
ROPE_DIM = 32
EPS = 1e-6
NEG = -1e30
FORCE_BONUS = 1e4

LANES = 128
V7X_VMEM_BYTES = 64 * 1024 * 1024
VMEM_LIMIT = V7X_VMEM_BYTES * 7 // 8


def _params(sem):
    return pltpu.CompilerParams(dimension_semantics=sem, vmem_limit_bytes=VMEM_LIMIT)


def _dot(a, b):
    return jnp.dot(a, b, preferred_element_type=F32)


def _dot_nt(a, b):
    return lax.dot_general(a, b, (((1,), (1,)), ((), ())), preferred_element_type=F32)


def _dot_tn(a, b):
    return lax.dot_general(a, b, (((0,), (0,)), ((), ())), preferred_element_type=F32)


def _split3(x):
    hi = x.astype(BF16)
    r1 = x - hi.astype(F32)
    mid = r1.astype(BF16)
    lo = (r1 - mid.astype(F32)).astype(BF16)
    return hi, mid, lo


def _sigmoid(x):
    return 0.5 * jnp.tanh(0.5 * x) + 0.5


def _silu(x):
    return x * _sigmoid(x)


def _softplus(x):
    return jnp.maximum(x, 0.0) + jnp.log1p(jnp.exp(-jnp.abs(x)))


def _rmsnorm_kernel(x_ref, g_ref, o_ref):
    x = x_ref[...]
    ms = jnp.mean(x * x, axis=-1, keepdims=True)
    o_ref[...] = (x * lax.rsqrt(ms + EPS) * g_ref[...]).astype(o_ref.dtype)


def rmsnorm(x, g, tm=256):
    s, d = x.shape
    return pl.pallas_call(
        _rmsnorm_kernel,
        grid=(s // tm,),
        in_specs=[pl.BlockSpec((tm, d), lambda i: (i, 0)),
                  pl.BlockSpec((1, d), lambda i: (0, 0))],
        out_specs=pl.BlockSpec((tm, d), lambda i: (i, 0)),
        out_shape=jax.ShapeDtypeStruct((s, d), BF16),
        compiler_params=_params(("parallel",)),
        name="rmsnorm",
    )(x, g.reshape(1, d))


def _row_tile(s, want):
    return min(s, want)


def _w_spec(w, layer, k, tn, col0=0):
    if w.ndim == 2:
        return pl.BlockSpec((k, tn), lambda i, j: (0, j + col0))
    return pl.BlockSpec((None, k, tn), lambda i, j: (layer, 0, j + col0))


def _ffn_in_kernel(x_ref, wa_ref, wb_ref, wo_ref, o_ref, wob_ref):
    x = x_ref[...]
    a = _dot(x, wa_ref[...].astype(BF16))
    b = _dot(x, wb_ref[...].astype(BF16))
    o_ref[...] = (_silu(a) * b).astype(o_ref.dtype)
    wob_ref[...] = wo_ref[...].astype(BF16)


def ffn_in(xn, w_in, w_out, layer, tm=1024, tn=256):
    s, k = xn.shape
    f = w_in.shape[-1] // 2
    tm = _row_tile(s, tm)
    nb = f // tn
    n_steps = (s // tm) * nb
    fo, d = w_out.shape[-2:]
    assert fo % n_steps == 0 and (fo // n_steps) % 16 == 0, (fo, n_steps)
    rows = fo // n_steps
    return pl.pallas_call(
        _ffn_in_kernel,
        grid=(s // tm, nb),
        in_specs=[pl.BlockSpec((tm, k), lambda i, j: (i, 0)),
                  _w_spec(w_in, layer, k, tn),
                  _w_spec(w_in, layer, k, tn, nb),
                  pl.BlockSpec((None, rows, d), lambda i, j: (layer, i * nb + j, 0))],
        out_specs=[pl.BlockSpec((tm, tn), lambda i, j: (i, j)),
                   pl.BlockSpec((rows, d), lambda i, j: (i * nb + j, 0))],
        out_shape=[jax.ShapeDtypeStruct((s, f), BF16), jax.ShapeDtypeStruct((fo, d), BF16)],
        compiler_params=_params(("parallel", "arbitrary")),
        name="ffn_in",
    )(xn, w_in, w_in, w_out)


def _mm_res_bf16_kernel(a_ref, w_ref, r_ref, o_ref, *, scale):
    o_ref[...] = r_ref[...] + scale * _dot(a_ref[...], w_ref[...])


def mm_res_bf16(a, w, res, scale, tm=512, tn=512):
    s, k = a.shape
    n = w.shape[1]
    tm = _row_tile(s, tm)
    return pl.pallas_call(
        functools.partial(_mm_res_bf16_kernel, scale=scale),
        grid=(s // tm, n // tn),
        in_specs=[pl.BlockSpec((tm, k), lambda i, j: (i, 0)),
                  pl.BlockSpec((k, tn), lambda i, j: (0, j)),
                  pl.BlockSpec((tm, tn), lambda i, j: (i, j))],
        out_specs=pl.BlockSpec((tm, tn), lambda i, j: (i, j)),
        out_shape=jax.ShapeDtypeStruct((s, n), F32),
        compiler_params=_params(("parallel", "arbitrary")),
        name="mm_res_bf16",
    )(a, w, res)


def _mm_plain_kernel(a_ref, w_ref, *rest, act):
    o_ref = rest[len(rest) // 2]
    acc = _dot(a_ref[...], w_ref[...].astype(BF16))
    if act == "sigmoid":
        acc = _sigmoid(acc)
    o_ref[...] = acc.astype(o_ref.dtype)
    if len(rest) == 3:
        rest[2][...] = rest[0][...].astype(BF16)


def mm_plain(a, w, layer, n, tn, tm=1024, act=None, out_dtype=F32, cast_w=None):
    s, k = a.shape
    tm = _row_tile(s, tm)
    grid = (s // tm, n // tn)
    in_specs = [pl.BlockSpec((tm, k), lambda i, j: (i, 0)), _w_spec(w, layer, k, tn)]
    out_specs = [pl.BlockSpec((tm, tn), lambda i, j: (i, j))]
    out_shape = [jax.ShapeDtypeStruct((s, n), out_dtype)]
    args = [a, w]
    if cast_w is not None:
        k2, n2 = cast_w.shape[-2:]
        n_steps = grid[0] * grid[1]
        assert k2 % n_steps == 0 and (k2 // n_steps) % 16 == 0, (k2, n_steps)
        rows = k2 // n_steps
        in_specs.append(pl.BlockSpec((None, rows, n2), lambda i, j: (layer, i * grid[1] + j, 0)))
        out_specs.append(pl.BlockSpec((rows, n2), lambda i, j: (i * grid[1] + j, 0)))
        out_shape.append(jax.ShapeDtypeStruct((k2, n2), BF16))
        args.append(cast_w)
    res = pl.pallas_call(
        functools.partial(_mm_plain_kernel, act=act),
        grid=grid,
        in_specs=in_specs,
        out_specs=out_specs,
        out_shape=out_shape,
        compiler_params=_params(("parallel", "arbitrary")),
        name="mm_plain",
    )(*args)
    return res if cast_w is not None else res[0]


def rope_tables(pos):
    half = ROPE_DIM // 2
    inv = ROPE_THETA ** (-jnp.arange(half, dtype=F32) / half)
    ang = pos.astype(F32)[:, None] * inv[None, :]
    cos, sin = jnp.cos(ang), jnp.sin(ang)
    n = pos.shape[0]
    ones = jnp.ones((n, HEAD_DIM - ROPE_DIM), F32)
    zeros = jnp.zeros((n, HEAD_DIM - ROPE_DIM), F32)
    z16 = jnp.zeros((n, half), F32)
    cos_t = jnp.concatenate([cos, cos, ones], axis=1)
    sin_a = jnp.concatenate([z16, sin, zeros], axis=1)
    sin_b = jnp.concatenate([-sin, z16, zeros], axis=1)
    return cos_t, sin_a, sin_b


def _norm_rope(x, g, cos_t, sin_a, sin_b):
    ms = jnp.mean(x * x, axis=-1, keepdims=True)
    y = x * lax.rsqrt(ms + EPS) * g
    half = ROPE_DIM // 2
    return (y * cos_t + pltpu.roll(y, half, 1) * sin_a
            + pltpu.roll(y, HEAD_DIM - half, 1) * sin_b)


def _head_norm_rope_kernel(x_ref, g_ref, c_ref, sa_ref, sb_ref, o_ref, *, nh, scale):
    g = g_ref[...]
    c, sa, sb = c_ref[...], sa_ref[...], sb_ref[...]
    ones = jnp.ones((HEAD_DIM, HEAD_DIM), BF16)
    half = ROPE_DIM // 2
    for h in range(nh):
        x = x_ref[:, h * HEAD_DIM:(h + 1) * HEAD_DIM]
        ssq = sum(_dot(piece, ones) for piece in _split3(x * x))
        y = x * lax.rsqrt(ssq * (1.0 / HEAD_DIM) + EPS) * g
        y = y * c + pltpu.roll(y, half, 1) * sa + pltpu.roll(y, HEAD_DIM - half, 1) * sb
        o_ref[:, h * HEAD_DIM:(h + 1) * HEAD_DIM] = (y * scale).astype(o_ref.dtype)


def head_norm_rope(x, col0, nh, g, tables, scale=1.0, tm=256):
    s = x.shape[0]
    w = nh * HEAD_DIM
    cb = col0 // w
    tm = _row_tile(s, tm)
    tab_spec = pl.BlockSpec((tm, HEAD_DIM), lambda i: (i, 0))
    return pl.pallas_call(
        functools.partial(_head_norm_rope_kernel, nh=nh, scale=scale),
        grid=(s // tm,),
        in_specs=[pl.BlockSpec((tm, w), lambda i: (i, cb)),
                  pl.BlockSpec((1, HEAD_DIM), lambda i: (0, 0)),
                  tab_spec, tab_spec, tab_spec],
        out_specs=pl.BlockSpec((tm, w), lambda i: (i, 0)),
        out_shape=jax.ShapeDtypeStruct((s, w), BF16),
        compiler_params=_params(("parallel",)),
        name="head_norm_rope",
    )(x, g.reshape(1, HEAD_DIM), *tables)


def _compress_kernel(t_ref, pe_ref, w1_ref, w2_ref, g_ref, c_ref, sa_ref, sb_ref, o_ref, *, is_key):
    half = CMP_STRIDE * HEAD_DIM
    t = t_ref[0]
    top = (t + pe_ref[:, :half]).astype(BF16)
    bot = (t + pe_ref[:, half:]).astype(BF16)
    p = _dot(top, w1_ref[:half, :].astype(BF16))
    q = _dot(bot, w1_ref[half:, :].astype(BF16))
    n = t.shape[0]
    hid = p + pltpu.roll(q, n - 1, 0)
    out = _dot(_silu(hid).astype(BF16), w2_ref[...].astype(BF16))
    if is_key:
        out = _norm_rope(out, g_ref[...], c_ref[...], sa_ref[...], sb_ref[...])
    o_ref[0] = out.astype(o_ref.dtype)


def compress(t16, pe, w1, w2, g, tables, is_key):
    ng, n, wid = t16.shape
    tab_spec = pl.BlockSpec((n, HEAD_DIM), lambda i: (0, 0))
    return pl.pallas_call(
        functools.partial(_compress_kernel, is_key=is_key),
        grid=(ng,),
        in_specs=[pl.BlockSpec((1, n, wid), lambda i: (i, 0, 0)),
                  pl.BlockSpec((1, 2 * wid), lambda i: (0, 0)),
                  pl.BlockSpec((2 * wid, HEAD_DIM), lambda i: (0, 0)),
                  pl.BlockSpec((HEAD_DIM, HEAD_DIM), lambda i: (0, 0)),
                  pl.BlockSpec((1, HEAD_DIM), lambda i: (0, 0)),
                  tab_spec, tab_spec, tab_spec],
        out_specs=pl.BlockSpec((1, n, HEAD_DIM), lambda i: (i, 0, 0)),
        out_shape=jax.ShapeDtypeStruct((ng, n, HEAD_DIM), BF16),
        compiler_params=_params(("parallel",)),
        name="compress_k" if is_key else "compress_v",
    )(t16, pe.reshape(1, 2 * wid), w1, w2, g.reshape(1, HEAD_DIM), *tables)


def _conv_silu(x, carry_ref, w_ref, b_ref, first):
    @pl.when(first)
    def _():
        carry_ref[...] = jnp.zeros_like(carry_ref)

    full = jnp.concatenate([carry_ref[...], x], axis=0)
    n = x.shape[0]
    acc = b_ref[...] + w_ref[CONV_WIDTH - 1:CONV_WIDTH, :] * x
    for k in range(1, CONV_WIDTH):
        acc = acc + w_ref[CONV_WIDTH - 1 - k:CONV_WIDTH - k, :] * full[8 - k:8 - k + n]
    carry_ref[...] = x[n - 8:, :]
    return _silu(acc)


def _ssd_kernel(z_ref, x_ref, b_ref, c_ref, dt_ref, dtt_ref,
                cwx_ref, cwb_ref, cwc_ref, cbx_ref, cbb_ref, cbc_ref,
                bias_ref, biast_ref, alog_ref, alogt_ref, d_ref, ng_ref, spread_ref,
                o_ref,
                state_ref, xs_ref, y_ref, dtx_ref, ecx_ref, tex_ref, cx_ref, cb_ref, cc_ref):
    ci = pl.program_id(1)
    first = ci == 0
    L = CHUNK

    @pl.when(first)
    def _():
        state_ref[...] = jnp.zeros_like(state_ref)

    xg = _conv_silu(x_ref[...], cx_ref, cwx_ref, cbx_ref, first)
    bm = _conv_silu(b_ref[...], cb_ref, cwb_ref, cbb_ref, first)
    cm = _conv_silu(c_ref[...], cc_ref, cwc_ref, cbc_ref, first)

    dt = _softplus(dt_ref[0] + bias_ref[0])
    dtt = _softplus(dtt_ref[0] + biast_ref[0])
    a = dt * (-jnp.exp(alog_ref[0]))
    at = dtt * (-jnp.exp(alogt_ref[0]))

    row = lax.broadcasted_iota(jnp.int32, (L, L), 0)
    col = lax.broadcasted_iota(jnp.int32, (L, L), 1)
    causal = row >= col
    tril = causal.astype(BF16)
    triu = (row <= col).astype(BF16)
    cum = sum(_dot(tril, p) for p in _split3(a))
    cumt = sum(_dot(p, triu) for p in _split3(at))
    ecum = jnp.exp(cum)
    last = cum[L - 1:L, :]
    toend = jnp.exp(last - cum)
    elast = jnp.exp(last)

    bmb = bm.astype(BF16)
    cmb = cm.astype(BF16)
    cbmat = _dot_nt(cmb, bmb)
    yoff = _dot(cmb, state_ref[...].astype(BF16))

    spread = spread_ref[...]

    def per_channel(v):
        return sum(_dot(piece, spread) for piece in _split3(v))

    dtx_ref[...] = per_channel(dt)
    ecx_ref[...] = per_channel(ecum)
    tex_ref[...] = per_channel(toend)

    lane = lax.broadcasted_iota(jnp.int32, (L, LANES), 1)
    lo = lane < SSM_HEAD_DIM
    lane1 = lax.broadcasted_iota(jnp.int32, (1, LANES), 1)
    lo1 = lane1 < SSM_HEAD_DIM
    for k in range(SSM_HPG // 2):
        h0, h1 = 2 * k, 2 * k + 1
        sl = slice(k * LANES, (k + 1) * LANES)
        xp = xg[:, sl]
        xdt = xp * dtx_ref[:, sl]
        g0 = (cbmat * jnp.exp(jnp.where(causal, cum[:, h0:h0 + 1] - cumt[h0:h0 + 1, :], -jnp.inf))).astype(BF16)
        g1 = (cbmat * jnp.exp(jnp.where(causal, cum[:, h1:h1 + 1] - cumt[h1:h1 + 1, :], -jnp.inf))).astype(BF16)
        ydiag = (_dot(g0, jnp.where(lo, xdt, 0.0).astype(BF16))
                 + _dot(g1, jnp.where(lo, 0.0, xdt).astype(BF16)))
        y = ydiag + yoff[:, sl] * ecx_ref[:, sl] + d_ref[:, sl] * xp
        y_ref[:, sl] = y * _silu(z_ref[:, sl])
        xs_ref[:, sl] = (xdt * tex_ref[:, sl]).astype(BF16)
        state_ref[:, sl] = state_ref[:, sl] * jnp.where(lo1, elast[:, h0:h0 + 1], elast[:, h1:h1 + 1])

    state_ref[...] += _dot_tn(bmb, xs_ref[...])

    y = y_ref[...]
    ms = jnp.mean(y * y, axis=-1, keepdims=True)
    o_ref[...] = (y * lax.rsqrt(ms + EPS) * ng_ref[...]).astype(o_ref.dtype)


def ssd_mixer_core(zx, bcdt, conv_w, conv_b, dt_bias, a_log, d_skip, norm_g):
    s = zx.shape[0]
    nc = s // CHUNK
    gw, n, hpg, ng = SSM_GROUP_W, SSM_STATE, SSM_HPG, SSM_GROUPS
    dt_raw = bcdt[:, 2 * ng * n:2 * ng * n + ng * hpg]
    pad = ((0, 0), (0, 0), (0, LANES - hpg))
    dt_g = jnp.pad(dt_raw.reshape(s, ng, hpg).transpose(1, 0, 2), pad)
    dt_gt = dt_raw.reshape(s, ng, hpg).transpose(1, 2, 0)

    def per_head(v):
        return jnp.pad(v.reshape(ng, 1, hpg), pad), v.reshape(ng, hpg, 1)

    bias, biast = per_head(dt_bias)
    alog, alogt = per_head(a_log)
    d_ch = jnp.repeat(d_skip, SSM_HEAD_DIM).reshape(1, D_INNER)
    spread = jnp.asarray(np.kron(np.eye(LANES, hpg), np.ones((1, SSM_HEAD_DIM))), BF16)
    cb2 = conv_b.reshape(1, -1)
    xoff, boff, coff = D_INNER // gw, D_INNER // n, D_INNER // n + ng
    head_spec = pl.BlockSpec((1, 1, LANES), lambda g, c: (g, 0, 0))
    headt_spec = pl.BlockSpec((1, hpg, 1), lambda g, c: (g, 0, 0))
    return pl.pallas_call(
        _ssd_kernel,
        grid=(ng, nc),
        in_specs=[
            pl.BlockSpec((CHUNK, gw), lambda g, c: (c, g)),
            pl.BlockSpec((CHUNK, gw), lambda g, c: (c, xoff + g)),
            pl.BlockSpec((CHUNK, n), lambda g, c: (c, g)),
            pl.BlockSpec((CHUNK, n), lambda g, c: (c, ng + g)),
            pl.BlockSpec((1, CHUNK, LANES), lambda g, c: (g, c, 0)),
            pl.BlockSpec((1, hpg, CHUNK), lambda g, c: (g, 0, c)),
            pl.BlockSpec((CONV_WIDTH, gw), lambda g, c: (0, g)),
            pl.BlockSpec((CONV_WIDTH, n), lambda g, c: (0, boff + g)),
            pl.BlockSpec((CONV_WIDTH, n), lambda g, c: (0, coff + g)),
            pl.BlockSpec((1, gw), lambda g, c: (0, g)),
            pl.BlockSpec((1, n), lambda g, c: (0, boff + g)),
            pl.BlockSpec((1, n), lambda g, c: (0, coff + g)),
            head_spec, headt_spec, head_spec, headt_spec,
            pl.BlockSpec((1, gw), lambda g, c: (0, g)),
            pl.BlockSpec((1, gw), lambda g, c: (0, g)),
            pl.BlockSpec((LANES, gw), lambda g, c: (0, 0)),
        ],
        out_specs=pl.BlockSpec((CHUNK, gw), lambda g, c: (c, g)),
        out_shape=jax.ShapeDtypeStruct((s, D_INNER), BF16),
        scratch_shapes=[
            pltpu.VMEM((n, gw), F32),
            pltpu.VMEM((CHUNK, gw), BF16),
            pltpu.VMEM((CHUNK, gw), F32),
            pltpu.VMEM((CHUNK, gw), F32),
            pltpu.VMEM((CHUNK, gw), F32),
            pltpu.VMEM((CHUNK, gw), F32),
            pltpu.VMEM((8, gw), F32),
            pltpu.VMEM((8, n), F32),
            pltpu.VMEM((8, n), F32),
        ],
        compiler_params=_params(("parallel", "arbitrary")),
        name="ssd",
    )(zx, zx, bcdt, bcdt, dt_g, dt_gt, conv_w, conv_w, conv_w, cb2, cb2, cb2,
      bias, biast, alog, alogt, d_ch, norm_g.reshape(1, -1), spread)


SEL_TILE = 256
SEL_SHIFT = SEL_BLOCK.bit_length() - 1
WIN_TILES = WINDOW // Q_BLOCK + 1
WIN_KEYS = WIN_TILES * Q_BLOCK
SUM_FLOOR = 1e-30


def _selection_matrix(n_cmp_pad, n_sblk):
    ratio = SEL_BLOCK // CMP_STRIDE
    offs = np.arange(-(CMP_BLOCK // CMP_STRIDE), ratio + 1)
    ov = np.clip(np.minimum(offs * CMP_STRIDE + CMP_BLOCK, SEL_BLOCK) - np.maximum(offs * CMP_STRIDE, 0), 0, None)
    m = np.zeros((n_cmp_pad, n_sblk), np.float32)
    for o, w in zip(offs, ov):
        if w > 0:
            for j in range(n_sblk):
                c = j * ratio + o
                if 0 <= c < n_cmp_pad - 1:
                    m[c, j] = w / CMP_BLOCK
    return m


def _rows(r):
    return slice(r * Q_BLOCK, (r + 1) * Q_BLOCK)


def _lane_fold(x, op):
    out = x[:, :LANES]
    for f in range(1, x.shape[1] // LANES):
        out = op(out, x[:, f * LANES:(f + 1) * LANES])
    return out


def _tile_lanes(x, width):
    return jnp.concatenate([x] * (width // LANES), axis=1)


def _row_stat(x, op, reduce):
    return jnp.broadcast_to(reduce(_lane_fold(x, op), axis=-1, keepdims=True), (x.shape[0], LANES))


def _nsa_kernel(q_ref, gate_ref, kc_ref, vc_ref, ks_ref, vs_ref, kw_ref, vw_ref, selmt_ref, shift_ref,
                gspread_ref, o_ref,
                selt_ref, gexp_ref, bias_ref, q_scr, s_ref, p_ref, s2_ref, p2_ref, b2_ref, psum_ref, stm_ref, stl_ref, wl_ref,
                mx_ref, lacc_ref, acc_ref, oc_ref, ow_ref):
    qb = pl.program_id(1)
    t0 = qb * Q_BLOCK
    R, QB, D = Q_PER_KV, Q_BLOCK, HEAD_DIM
    n_cmp = kc_ref.shape[1]
    n_sblk = selmt_ref.shape[0]

    tok = t0 + lax.broadcasted_iota(jnp.int32, (QB, 1), 0)
    for r in range(R):
        q_scr[_rows(r), :] = q_ref[:, r * D:(r + 1) * D]

    cend = lax.broadcasted_iota(jnp.int32, (QB, n_cmp), 1) * CMP_STRIDE + (CMP_BLOCK - 1)
    b2_ref[:, :n_cmp] = jnp.where(cend <= tok, 0.0, NEG)
    s_ref[:, :n_cmp] = _dot_nt(q_scr[...], kc_ref[0])
    for r in range(R):
        stm_ref[_rows(r)] = _row_stat(s_ref[_rows(r), :n_cmp] + b2_ref[:, :n_cmp], jnp.maximum, jnp.max)
    for r in range(R):
        e = jnp.exp(s_ref[_rows(r), :n_cmp] + b2_ref[:, :n_cmp] - _tile_lanes(stm_ref[_rows(r)], n_cmp))
        s_ref[_rows(r), :n_cmp] = e
        stl_ref[_rows(r)] = _row_stat(e, jnp.add, jnp.sum)
    has_key = tok >= CMP_BLOCK - 1
    psum_ref[...] = jnp.zeros(psum_ref.shape, F32)
    for r in range(R):
        inv = jnp.where(has_key, 1.0 / stl_ref[_rows(r)], 0.0)
        p = s_ref[_rows(r), :n_cmp] * _tile_lanes(inv, n_cmp)
        psum_ref[...] += p
        p_ref[_rows(r), :n_cmp] = p.astype(BF16)
    oc_ref[...] = _dot(p_ref[:, :n_cmp], vc_ref[0])
    psum = psum_ref[...]

    selmt = selmt_ref[...].astype(BF16)
    imp = sum(_dot_nt(selmt, piece) for piece in _split3(psum))
    jj = lax.broadcasted_iota(jnp.int32, (n_sblk, QB), 0)
    jf = jj.astype(F32)
    tok_t = t0 + lax.broadcasted_iota(jnp.int32, (1, QB), 1)
    cur = jnp.right_shift(tok_t, SEL_SHIFT)
    forced = (jj == 0) | (jj == cur) | (jj == cur - 1)
    score0 = jnp.where(jj <= cur, imp + jnp.where(forced, FORCE_BONUS, 0.0), NEG)
    n_top = min(N_SELECT, n_sblk)

    score = score0
    sel_t = jnp.zeros((n_sblk, QB), F32)
    for _ in range(n_top):
        hit = score == jnp.max(score, axis=0, keepdims=True)
        sel_t = jnp.where(hit, 1.0, sel_t)
        score = jnp.where(hit, -jnp.inf, score)
    selt_ref[...] = sel_t
    n_taken = jnp.max(jnp.sum(sel_t, axis=0, keepdims=True))

    @pl.when(n_taken > n_top)
    def _():
        score = score0
        sel_x = jnp.zeros((n_sblk, QB), F32)
        for _ in range(n_top):
            mx = jnp.max(score, axis=0, keepdims=True)
            idx = jnp.min(jnp.where(score == mx, jf, float(n_sblk)), axis=0, keepdims=True)
            hit = jf == idx
            sel_x = jnp.where(hit, 1.0, sel_x)
            score = jnp.where(hit, -jnp.inf, score)
        selt_ref[...] = sel_x

    selb = selt_ref[...].T.astype(BF16)

    kws, vws = [], []
    for i in range(WIN_TILES):
        tile = qb - (WIN_TILES - 1) + i
        k0 = pl.multiple_of(jnp.maximum(tile, 0) * QB, QB)
        kws.append(kw_ref[pl.ds(k0, QB), :])
        vws.append(vw_ref[pl.ds(k0, QB), :])
        spos = tile * QB + lax.broadcasted_iota(jnp.int32, (QB, QB), 1)
        ok = (spos <= tok) & (spos > tok - WINDOW) & (spos >= 0)
        b2_ref[:, i * QB:(i + 1) * QB] = jnp.where(ok, 0.0, NEG)
    s_ref[:, :WIN_KEYS] = _dot_nt(q_scr[...], jnp.concatenate(kws, axis=0))
    for r in range(R):
        stm_ref[_rows(r)] = _row_stat(s_ref[_rows(r), :WIN_KEYS] + b2_ref[:, :WIN_KEYS], jnp.maximum, jnp.max)
    for r in range(R):
        e = jnp.exp(s_ref[_rows(r), :WIN_KEYS] + b2_ref[:, :WIN_KEYS] - _tile_lanes(stm_ref[_rows(r)], WIN_KEYS))
        p_ref[_rows(r), :WIN_KEYS] = e.astype(BF16)
        wl_ref[_rows(r)] = _row_stat(e, jnp.add, jnp.sum)
    ow_ref[...] = _dot(p_ref[:, :WIN_KEYS], jnp.concatenate(vws, axis=0))

    T = SEL_TILE
    n_pairs = (t0 + QB + 2 * T - 1) // (2 * T)
    last_tile = ks_ref.shape[0] // T - 1
    shift = shift_ref[...]

    def tile_start(kt):
        return pl.multiple_of(jnp.minimum(kt, last_tile) * T, T)

    def scores(kt):
        return _dot_nt(q_scr[...], ks_ref[pl.ds(tile_start(kt), T), :])

    def build_bias(kt):
        k0 = kt * T
        kpos = k0 + lax.broadcasted_iota(jnp.int32, (n_sblk, T), 1)
        blk = lax.broadcasted_iota(jnp.int32, (n_sblk, T), 0)
        expand = jnp.where(jnp.right_shift(kpos, SEL_SHIFT) == blk, 1.0, 0.0).astype(BF16)
        picked = _dot(selb, expand)
        kq = k0 + lax.broadcasted_iota(jnp.int32, (QB, T), 1)
        bias_ref[kt] = jnp.where((picked > 0.5) & (kq <= tok), -shift, NEG)

    def exp_tile(buf, pbuf, kt, use_max):
        for r in range(R):
            x = buf[_rows(r), :T] + bias_ref[kt]
            if use_max:
                x = x - _tile_lanes(mx_ref[_rows(r)], T)
            e = jnp.exp(x)
            lacc_ref[_rows(r)] += _lane_fold(e, jnp.add)
            pbuf[_rows(r), :T] = e.astype(BF16)

    def sum_pass(use_max, build):
        lacc_ref[...] = jnp.zeros(lacc_ref.shape, F32)
        acc_ref[...] = jnp.zeros(acc_ref.shape, F32)

        def body(j, c):
            a = 2 * j
            s2_ref[...] = scores(a + 1)
            if build:
                build_bias(a + 2)
                build_bias(a + 3)
            exp_tile(s_ref, p_ref, a, use_max)
            acc_ref[...] += _dot(p_ref[:, :T], vs_ref[pl.ds(tile_start(a), T), :])
            s_ref[:, :T] = scores(a + 2)
            exp_tile(s2_ref, p2_ref, a + 1, use_max)
            acc_ref[...] += _dot(p2_ref[...], vs_ref[pl.ds(tile_start(a + 1), T), :])
            return c

        s_ref[:, :T] = scores(0)
        if build:
            build_bias(0)
            build_bias(1)
        lax.fori_loop(0, n_pairs, body, 0)

    sum_pass(use_max=False, build=True)
    l_min = jnp.min(jnp.sum(lacc_ref[...], axis=-1, keepdims=True))

    @pl.when(l_min < SUM_FLOOR)
    def _():
        mx_ref[...] = jnp.full(mx_ref.shape, NEG, F32)

        def fold_max(buf, kt):
            for r in range(R):
                m = _lane_fold(buf[_rows(r), :T] + bias_ref[kt], jnp.maximum)
                mx_ref[_rows(r)] = jnp.maximum(mx_ref[_rows(r)], m)

        def max_body(j, c):
            a = 2 * j
            s2_ref[...] = scores(a + 1)
            fold_max(s_ref, a)
            s_ref[:, :T] = scores(a + 2)
            fold_max(s2_ref, a + 1)
            return c

        s_ref[:, :T] = scores(0)
        lax.fori_loop(0, n_pairs, max_body, 0)
        for r in range(R):
            mx_ref[_rows(r)] = jnp.broadcast_to(jnp.max(mx_ref[_rows(r)], axis=-1, keepdims=True), (QB, LANES))
        sum_pass(use_max=True, build=False)

    gspread = gspread_ref[...]
    gexp_ref[...] = sum(_dot(piece, gspread) for piece in _split3(gate_ref[...]))
    for r in range(R):
        osel = acc_ref[_rows(r)] * (1.0 / jnp.sum(lacc_ref[_rows(r)], axis=-1, keepdims=True))
        ow = ow_ref[_rows(r)] * (1.0 / wl_ref[_rows(r)])
        gc, gs, gw = (gexp_ref[:, (3 * r + b) * LANES:(3 * r + b + 1) * LANES] for b in range(3))
        o = gc * oc_ref[_rows(r)] + gs * osel + gw * ow
        o_ref[:, r * D:(r + 1) * D] = o.astype(o_ref.dtype)


def nsa_attention(q, gates, kc, vc, ks, vs, kw, vw, score_bound):
    s = q.shape[0]
    shift = jnp.full((1, SEL_TILE), score_bound, F32)
    n_gate = 3 * Q_PER_KV
    gspread = jnp.asarray(np.kron(np.eye(LANES, n_gate), np.ones((1, LANES))), BF16)
    nqb = s // Q_BLOCK
    n_cmp = kc.shape[1]
    n_sblk = s // SEL_BLOCK
    gw = Q_PER_KV * HEAD_DIM
    selmt = jnp.asarray(_selection_matrix(n_cmp, n_sblk).T)
    kv_spec = pl.BlockSpec((s, HEAD_DIM), lambda g, i: (0, g))
    cmp_spec = pl.BlockSpec((1, n_cmp, HEAD_DIM), lambda g, i: (g, 0, 0))
    rows = Q_PER_KV * Q_BLOCK
    wide = max(WIN_KEYS, n_cmp, SEL_TILE)
    return pl.pallas_call(
        _nsa_kernel,
        grid=(KV_GROUPS, nqb),
        in_specs=[pl.BlockSpec((Q_BLOCK, gw), lambda g, i: (i, g)),
                  pl.BlockSpec((Q_BLOCK, LANES), lambda g, i: (i, g)),
                  cmp_spec, cmp_spec, kv_spec, kv_spec, kv_spec, kv_spec,
                  pl.BlockSpec((n_sblk, n_cmp), lambda g, i: (0, 0)),
                  pl.BlockSpec((1, SEL_TILE), lambda g, i: (0, 0)),
                  pl.BlockSpec((LANES, n_gate * LANES), lambda g, i: (0, 0))],
        out_specs=pl.BlockSpec((Q_BLOCK, gw), lambda g, i: (i, g)),
        out_shape=jax.ShapeDtypeStruct((s, ATTN_HEADS * HEAD_DIM), BF16),
        scratch_shapes=[
            pltpu.VMEM((n_sblk, Q_BLOCK), F32),
            pltpu.VMEM((Q_BLOCK, n_gate * LANES), F32),
            pltpu.VMEM((s // SEL_TILE + 2, Q_BLOCK, SEL_TILE), F32),
            pltpu.VMEM((rows, HEAD_DIM), BF16),
            pltpu.VMEM((rows, wide), F32),
            pltpu.VMEM((rows, wide), BF16),
            pltpu.VMEM((rows, SEL_TILE), F32),
            pltpu.VMEM((rows, SEL_TILE), BF16),
            pltpu.VMEM((Q_BLOCK, wide), F32),
            pltpu.VMEM((Q_BLOCK, n_cmp), F32),
            pltpu.VMEM((rows, LANES), F32),
            pltpu.VMEM((rows, LANES), F32),
            pltpu.VMEM((rows, LANES), F32),
            pltpu.VMEM((rows, LANES), F32),
            pltpu.VMEM((rows, LANES), F32),
            pltpu.VMEM((rows, HEAD_DIM), F32),
            pltpu.VMEM((rows, HEAD_DIM), F32),
            pltpu.VMEM((rows, HEAD_DIM), F32),
        ],
        compiler_params=_params(("parallel", "arbitrary")),
        name="nsa_attention",
    )(q, gates, kc, vc, ks, vs, kw, vw, selmt, shift, gspread)


def _ffn(h, norm_g, w_in, w_out, layer):
    act, w_out_bf16 = ffn_in(rmsnorm(h, norm_g[layer]), w_in, w_out, layer)
    return mm_res_bf16(act, w_out_bf16, h, 0.5)


def _mamba(h, norm_g, w_in, conv_w, conv_b, dt_bias, a_log, d_skip, ssm_norm, w_out, layer):
    u = rmsnorm(h, norm_g)
    zx_cols = 2 * D_INNER
    zx, w_out_bf16 = mm_plain(u, w_in, layer, zx_cols, 512, cast_w=w_out)
    w_tail = w_in[layer, :, zx_cols:]
    w_tail = jnp.pad(w_tail, ((0, 0), (0, -w_tail.shape[1] % 256)))
    bcdt = mm_plain(u, w_tail, 0, w_tail.shape[1], 256)
    y = ssd_mixer_core(zx, bcdt, conv_w[layer], conv_b[layer], dt_bias[layer], a_log[layer], d_skip[layer],
                       ssm_norm[layer])
    return mm_res_bf16(y, w_out_bf16, h, 1.0)


def _shared_kv(h, kv_norm, kv_w, cmp_pe_k, cmp_w1_k, cmp_w2_k, cmp_pe_v, cmp_w1_v, cmp_w2_v,
               k_norm_cmp, k_norm_slc, k_norm_win):
    s = h.shape[0]
    gd = KV_GROUPS * HEAD_DIM
    kv = mm_plain(rmsnorm(h, kv_norm), kv_w, 0, 6 * gd, 512)
    n16 = s // CMP_STRIDE

    def blocks16(t):
        return t.reshape(n16, CMP_STRIDE, KV_GROUPS, HEAD_DIM).transpose(2, 0, 1, 3).reshape(
            KV_GROUPS, n16, CMP_STRIDE * HEAD_DIM)

    cmp_tables = rope_tables(jnp.arange(n16) * CMP_STRIDE + (CMP_BLOCK - 1))
    pos_tables = rope_tables(jnp.arange(s))
    kc = compress(blocks16(kv[:, 0:gd]), cmp_pe_k, cmp_w1_k, cmp_w2_k, k_norm_cmp, cmp_tables, True)
    vc = compress(blocks16(kv[:, gd:2 * gd]), cmp_pe_v, cmp_w1_v, cmp_w2_v, k_norm_cmp, cmp_tables, False)
    ks = head_norm_rope(kv, 2 * gd, KV_GROUPS, k_norm_slc, pos_tables)
    kw = head_norm_rope(kv, 4 * gd, KV_GROUPS, k_norm_win, pos_tables)
    vs = kv[:, 3 * gd:4 * gd].astype(BF16)
    vw = kv[:, 5 * gd:6 * gd].astype(BF16)
    k_bound = math.sqrt(HEAD_DIM) * jnp.max(jnp.abs(k_norm_slc))
    return kc, vc, ks, vs, kw, vw, pos_tables, k_bound


def _nsa(h, norm_g, w_qg, q_norm_g, w_o, shared, layer):
    kc, vc, ks, vs, kw, vw, pos_tables, k_bound = shared
    u = rmsnorm(h, norm_g)
    qd = ATTN_HEADS * HEAD_DIM
    per_g = Q_PER_KV * 3
    w_g = w_qg[layer, :, qd:].reshape(D_MODEL, KV_GROUPS, per_g)
    w_g = jnp.pad(w_g, ((0, 0), (0, 0), (0, LANES - per_g))).reshape(D_MODEL, KV_GROUPS * LANES)
    gates = mm_plain(u, w_g, 0, KV_GROUPS * LANES, KV_GROUPS * LANES, act="sigmoid")
    scale = HEAD_DIM ** -0.5
    q, w_o_bf16 = mm_plain(u, w_qg, layer, qd, 512, cast_w=w_o)
    qn = head_norm_rope(q, 0, ATTN_HEADS, q_norm_g[layer], pos_tables, scale=scale)
    q_bound = math.sqrt(HEAD_DIM) * scale * jnp.max(jnp.abs(q_norm_g[layer]))
    o = nsa_attention(qn, gates, kc, vc, ks, vs, kw, vw, 1.01 * q_bound * k_bound)
    return mm_res_bf16(o, w_o_bf16, h, 1.0, tm=1024)


def kernel(x, ffn_a_norm, ffn_a_w_in, ffn_a_w_out, ffn_b_norm, ffn_b_w_in, ffn_b_w_out, mix_norm,
           ssm_w_in, ssm_conv_w, ssm_conv_b, ssm_dt_bias, ssm_a_log, ssm_d, ssm_norm, ssm_w_out,
           kv_norm, kv_w, cmp_pe_k, cmp_w1_k, cmp_w2_k, cmp_pe_v, cmp_w1_v, cmp_w2_v,
           k_norm_cmp, k_norm_slc, k_norm_win, attn_w_qg, attn_q_norm, attn_w_o):
    b, s, d = x.shape
    n_layers = ffn_a_norm.shape[0]
    n_a = ssm_w_in.shape[0]
    outs = []
    for bi in range(b):
        h = x.reshape(s, d) if b == 1 else x[bi]
        shared = None
        for i in range(n_layers):
            h = _ffn(h, ffn_a_norm, ffn_a_w_in, ffn_a_w_out, i)
            if i < n_a:
                h = _mamba(h, mix_norm[i], ssm_w_in, ssm_conv_w, ssm_conv_b, ssm_dt_bias,
                           ssm_a_log, ssm_d, ssm_norm, ssm_w_out, i)
            else:
                h = _nsa(h, mix_norm[i], attn_w_qg, attn_q_norm, attn_w_o, shared, i - n_a)
            h = _ffn(h, ffn_b_norm, ffn_b_w_in, ffn_b_w_out, i)
            if i == n_a - 1:
                shared = _shared_kv(h, kv_norm, kv_w, cmp_pe_k, cmp_w1_k, cmp_w2_k, cmp_pe_v, cmp_w1_v,
                                    cmp_w2_v, k_norm_cmp, k_norm_slc, k_norm_win)
        outs.append(h)
    return outs[0].reshape(b, s, d) if b == 1 else jnp.stack(outs, axis=0)
```
